```python
import jax, jax.numpy as jnp
from jax import lax
import numpy as np


D_MODEL = 1024
BATCH = 32
SEQ = 2048
DEPTH = 1
DEC_BATCH = 2
DEC_SEQ = 8192
PAST_LEN = 128

D_MIX = D_MODEL
POOL_WIDTH = D_MIX // 2
POOL_WINDOWS = (2, 4, 8, 16)
POOL_GROUP = POOL_WIDTH // len(POOL_WINDOWS)
MLSTM_WIDTH = D_MIX - POOL_WIDTH
MLSTM_HEADS = 4
MLSTM_HEAD_DIM = MLSTM_WIDTH // MLSTM_HEADS
CHUNK = 128
N_GATE_COLS = 4 * MLSTM_HEADS
OFF_POOL = 0
OFF_Q = OFF_POOL + POOL_WIDTH
OFF_K = OFF_Q + MLSTM_WIDTH
OFF_V = OFF_K + MLSTM_WIDTH
OFF_O = OFF_V + MLSTM_WIDTH
OFF_G = OFF_O + MLSTM_WIDTH
P_IN = OFF_G + N_GATE_COLS
N_GROUPS = 4
EXPERTS_PER_GROUP = 8
N_EXPERTS = N_GROUPS * EXPERTS_PER_GROUP
TOP_K_INNER = 2
D_EXPERT = D_MODEL // 2
MOE_BLOCK = 128
EPS = 1e-6

kernel_name = "hymba_pool_mlstm_hmoe_encoder"


def rmsnorm(x, g):
    x32 = x.astype(jnp.float32)
    y = x32 * lax.rsqrt(jnp.mean(x32 * x32, axis=-1, keepdims=True) + EPS)
    return (y * g.astype(jnp.float32)).astype(x.dtype)


def multiscale_pool(u, w_pool, scale):
    B, S, C = u.shape
    u32 = u.astype(jnp.float32)
    cs = jnp.concatenate([jnp.zeros((B, 1, C), jnp.float32), jnp.cumsum(u32, axis=1)], axis=1)
    pos = jnp.arange(S)
    outs = []
    for g, w in enumerate(POOL_WINDOWS):
        sl = slice(g * POOL_GROUP, (g + 1) * POOL_GROUP)
        lo = jnp.clip(pos - w // 2, 0, S)
        hi = jnp.clip(pos + w // 2, 0, S)
        csg = cs[..., sl]
        cnt = (hi - lo).astype(jnp.float32)[None, :, None]
        d = (csg[:, hi] - csg[:, lo]) / cnt - u32[..., sl]
        outs.append(jnp.einsum('bsc,ce->bse', d, w_pool[g].astype(jnp.float32)))
    return jnp.concatenate(outs, axis=-1) * scale.astype(jnp.float32)


def mlstm_chunkwise(q, k, v, ig, lf):
    B, S, H, d = q.shape
    nc = S // CHUNK
    to_c = lambda a: a.reshape(B, nc, CHUNK, H, d).transpose(1, 0, 3, 2, 4)
    to_cg = lambda a: a.reshape(B, nc, CHUNK, H).transpose(1, 0, 3, 2)
    mask = jnp.tril(jnp.ones((CHUNK, CHUNK), bool))

    def step(carry, xs):
        C, n, m = carry
        qc, kc, vc, ic, fc = xs
        b = jnp.cumsum(fc, axis=-1)
        logw = jnp.where(mask, b[..., :, None] - b[..., None, :] + ic[..., None, :], -jnp.inf)
        inter_log = b + m[..., None]
        m_row = jnp.maximum(inter_log, jnp.max(logw, axis=-1))
        s = jnp.einsum('bhjd,bhsd->bhjs', qc, kc) * jnp.exp(logw - m_row[..., None])
        inter = jnp.exp(inter_log - m_row)
        num = jnp.einsum('bhjs,bhsd->bhjd', s, vc) + inter[..., None] * jnp.einsum('bhed,bhjd->bhje', C, qc)
        den = jnp.sum(s, axis=-1) + inter * jnp.einsum('bhd,bhjd->bhj', n, qc)
        h = num / jnp.maximum(jnp.abs(den), jnp.exp(-m_row))[..., None]
        bL = b[..., -1]
        logu = bL[..., None] - b + ic
        m_new = jnp.maximum(bL + m, jnp.max(logu, axis=-1))
        u = jnp.exp(logu - m_new[..., None])
        decay = jnp.exp(bL + m - m_new)
        C_new = decay[..., None, None] * C + jnp.einsum('bhs,bhse,bhsd->bhed', u, vc, kc)
        n_new = decay[..., None] * n + jnp.einsum('bhs,bhsd->bhd', u, kc)
        return (C_new, n_new, m_new), h

    init = (jnp.zeros((B, H, d, d), jnp.float32), jnp.zeros((B, H, d), jnp.float32),
            jnp.zeros((B, H), jnp.float32))
    _, hs = lax.scan(step, init, (to_c(q), to_c(k), to_c(v), to_cg(ig), to_cg(lf)))
    return hs.transpose(1, 0, 3, 2, 4).reshape(B, S, H, d)


def bidir_mlstm(proj, gate_b, head_g):
    B, S, _ = proj.shape
    H, d = MLSTM_HEADS, MLSTM_HEAD_DIM
    p32 = proj.astype(jnp.float32)
    q = p32[..., OFF_Q:OFF_K].reshape(B, S, H, d)
    k = p32[..., OFF_K:OFF_V].reshape(B, S, H, d) * (d ** -0.5)
    v = p32[..., OFF_V:OFF_O].reshape(B, S, H, d)
    o = p32[..., OFF_O:OFF_G]
    g = p32[..., OFF_G:P_IN] + gate_b.astype(jnp.float32)
    ig_f, ig_b = g[..., 0:H], g[..., H:2 * H]
    lf_f, lf_b = jax.nn.log_sigmoid(g[..., 2 * H:3 * H]), jax.nn.log_sigmoid(g[..., 3 * H:4 * H])
    h_f = mlstm_chunkwise(q, k, v, ig_f, lf_f)
    h_b = jnp.flip(mlstm_chunkwise(*(jnp.flip(a, axis=1) for a in (q, k, v, ig_b, lf_b))), axis=1)
    h = h_f + h_b
    h = h * lax.rsqrt(jnp.mean(h * h, axis=-1, keepdims=True) + EPS)
    h = h.reshape(B, S, MLSTM_WIDTH) * head_g.astype(jnp.float32)
    return jax.nn.sigmoid(o) * h


def hier_moe(xn, w_rg, b_rg, w_re, b_re, w_gate, w_up, w_down):
    B, S, D = xn.shape
    T = B * S
    xt = xn.reshape(T, D)
    gl = jnp.matmul(xt, w_rg).astype(jnp.float32) + b_rg.astype(jnp.float32)
    g_idx = jnp.argmax(gl, axis=-1)
    g_w = jnp.take_along_axis(jax.nn.softmax(gl, axis=-1), g_idx[:, None], axis=-1)[:, 0]
    el = (jnp.matmul(xt, w_re.reshape(D, N_EXPERTS)).astype(jnp.float32)
          .reshape(T, N_GROUPS, EXPERTS_PER_GROUP) + b_re.astype(jnp.float32))
    el_sel = jnp.take_along_axis(el, g_idx[:, None, None], axis=1)[:, 0]
    top_v, top_i = lax.top_k(el_sel, TOP_K_INNER)
    ew = jax.nn.softmax(top_v, axis=-1) * g_w[:, None]
    e = (g_idx[:, None] * EXPERTS_PER_GROUP + top_i).reshape(-1).astype(jnp.int32)
    tok = jnp.repeat(jnp.arange(T, dtype=jnp.int32), TOP_K_INNER)
    wt = ew.reshape(-1)
    order = jnp.argsort(e)
    se, stok, swt = e[order], tok[order], wt[order]
    counts = jnp.bincount(e, length=N_EXPERTS)
    starts = jnp.cumsum(counts) - counts
    pcounts = ((counts + MOE_BLOCK - 1) // MOE_BLOCK) * MOE_BLOCK
    pends = jnp.cumsum(pcounts)
    pstarts = pends - pcounts
    N = T * TOP_K_INNER
    dest = pstarts[se] + (jnp.arange(N) - starts[se])
    nb = (N + MOE_BLOCK - 1) // MOE_BLOCK + N_EXPERTS
    P = nb * MOE_BLOCK
    row_tok = jnp.full((P,), T, jnp.int32).at[dest].set(stok)
    xpad = jnp.concatenate([xt, jnp.zeros((1, D), xt.dtype)], axis=0)
    xrows = xpad[row_tok].reshape(nb, MOE_BLOCK, D)
    block_e = jnp.clip(jnp.searchsorted(pends, jnp.arange(nb) * MOE_BLOCK, side='right'), 0, N_EXPERTS - 1)

    def expert_block(args):
        xb, eb = args
        hb = jax.nn.silu(jnp.matmul(xb, w_gate[eb])) * jnp.matmul(xb, w_up[eb])
        return jnp.matmul(hb, w_down[eb])

    yrows = lax.map(expert_block, (xrows, block_e)).reshape(P, D)
    y = jax.ops.segment_sum(yrows[dest] * swt[:, None].astype(yrows.dtype), stok, num_segments=T)
    return y.reshape(B, S, D)


def encoder(x, norm1_g, w_in, pool_w, pool_scale, gate_b, head_norm_g, w_out, norm2_g,
            router_group_w, router_group_b, router_expert_w, router_expert_b,
            expert_w_gate, expert_w_up, expert_w_down, final_norm_g):
    for l in range(DEPTH):
        xn = rmsnorm(x, norm1_g[l])
        proj = jnp.matmul(xn, w_in[l])
        a = multiscale_pool(proj[..., OFF_POOL:OFF_Q], pool_w[l], pool_scale[l])
        m = bidir_mlstm(proj, gate_b[l], head_norm_g[l])
        mix = jnp.concatenate([a, m], axis=-1).astype(x.dtype)
        x = x + jnp.matmul(mix, w_out[l])
        xn2 = rmsnorm(x, norm2_g[l])
        x = x + hier_moe(xn2, router_group_w[l], router_group_b[l], router_expert_w[l],
                         router_expert_b[l], expert_w_gate[l], expert_w_up[l], expert_w_down[l])
    return rmsnorm(x, final_norm_g)


def setup_inputs(seed: int = 0) -> dict:
    key = jax.random.key(seed)
    ks = jax.random.split(key, 20)
    f32 = jnp.float32
    nrm = lambda k, shape, s: jax.random.normal(k, shape, f32) * s
    H = MLSTM_HEADS
    i_bias = nrm(ks[6], (DEPTH, 2 * H), 0.1)
    f_bias = jnp.tile(jnp.linspace(3.0, 6.0, H, dtype=f32), 2)[None, :] + nrm(ks[7], (DEPTH, 2 * H), 0.1)
    return {
        "x_prompt": nrm(ks[0], (BATCH, SEQ, D_MODEL), 1.0),
        "x_sample": nrm(ks[1], (DEC_BATCH, DEC_SEQ, D_MODEL), 1.0),
        "norm1_g": 1.0 + nrm(ks[2], (DEPTH, D_MODEL), 0.05),
        "w_in": nrm(ks[3], (DEPTH, D_MODEL, P_IN), D_MODEL ** -0.5),
        "pool_w": nrm(ks[4], (DEPTH, len(POOL_WINDOWS), POOL_GROUP, POOL_GROUP), POOL_GROUP ** -0.5),
        "pool_scale": 1.0 + nrm(ks[5], (DEPTH, POOL_WIDTH), 0.1),
        "gate_b": jnp.concatenate([i_bias, f_bias], axis=-1),
        "head_norm_g": 1.0 + nrm(ks[8], (DEPTH, MLSTM_WIDTH), 0.05),
        "w_out": nrm(ks[9], (DEPTH, D_MIX, D_MODEL), D_MIX ** -0.5),
        "norm2_g": 1.0 + nrm(ks[10], (DEPTH, D_MODEL), 0.05),
        "router_group_w": nrm(ks[11], (DEPTH, D_MODEL, N_GROUPS), D_MODEL ** -0.5),
        "router_group_b": nrm(ks[12], (DEPTH, N_GROUPS), 0.01),
        "router_expert_w": nrm(ks[13], (DEPTH, D_MODEL, N_GROUPS, EXPERTS_PER_GROUP), D_MODEL ** -0.5),
        "router_expert_b": nrm(ks[14], (DEPTH, N_GROUPS, EXPERTS_PER_GROUP), 0.01),
        "expert_w_gate": nrm(ks[15], (DEPTH, N_EXPERTS, D_MODEL, D_EXPERT), D_MODEL ** -0.5),
        "expert_w_up": nrm(ks[16], (DEPTH, N_EXPERTS, D_MODEL, D_EXPERT), D_MODEL ** -0.5),
        "expert_w_down": nrm(ks[17], (DEPTH, N_EXPERTS, D_EXPERT, D_MODEL), D_EXPERT ** -0.5),
        "final_norm_g": 1.0 + nrm(ks[18], (D_MODEL,), 0.05),
    }


def reference(x_prompt, x_sample, norm1_g, w_in, pool_w, pool_scale, gate_b, head_norm_g, w_out,
              norm2_g, router_group_w, router_group_b, router_expert_w, router_expert_b,
              expert_w_gate, expert_w_up, expert_w_down, final_norm_g):
    y_prompt = encoder(x_prompt, norm1_g, w_in, pool_w, pool_scale, gate_b, head_norm_g, w_out,
                       norm2_g, router_group_w, router_group_b, router_expert_w, router_expert_b,
                       expert_w_gate, expert_w_up, expert_w_down, final_norm_g)
    y_sample = encoder(x_sample, norm1_g, w_in, pool_w, pool_scale, gate_b, head_norm_g, w_out,
                       norm2_g, router_group_w, router_group_b, router_expert_w, router_expert_b,
                       expert_w_gate, expert_w_up, expert_w_down, final_norm_g)
    return (y_prompt, y_sample)
```

```python
import functools

import jax
import jax.numpy as jnp
from jax import lax
from jax.experimental import pallas as pl
from jax.experimental.pallas import tpu as pltpu

F32 = jnp.float32
BF16 = jnp.bfloat16
I32 = jnp.int32

D_MODEL = 1024
POOL_WIDTH = 512
POOL_WINDOWS = (2, 4, 8, 16)
POOL_GROUP = 128
POOL_HALO = 8
MLSTM_WIDTH = 512
HEADS = 4
HEAD_DIM = 128
CHUNK = 128
N_GATES = 16
N_GROUPS = 4
EPG = 8
N_EXPERTS = 32
D_EXPERT = 512
EPS = 1e-6
LANES = 128
ROUTER_ROWS = 8 + N_EXPERTS

TM_PROJ = 512
TM_MIX = 512
TM_ROWS = 256
BM_EXPERT = 256
VMEM_LIMIT = 56 * 1024 * 1024

HIGHEST = lax.Precision.HIGHEST


def _cparams(n_axes):
    return pltpu.CompilerParams(dimension_semantics=("arbitrary",) * n_axes,
                                vmem_limit_bytes=VMEM_LIMIT)


def _inproj_kernel(x_ref, g_ref, w_ref, wg_ref, gb_ref, u_ref, q_ref, k_ref, v_ref, o_ref, gt_ref):
    x = x_ref[...]
    inv = lax.rsqrt(jnp.mean(x * x, axis=-1, keepdims=True) + EPS)
    xn = (x * inv * g_ref[...]).astype(BF16)

    def sec(i):
        return jnp.dot(xn, w_ref[:, i * 512:(i + 1) * 512], preferred_element_type=F32)

    u_ref[...] = sec(0)
    q_ref[...] = sec(1).astype(BF16)
    k_ref[...] = (sec(2) * (HEAD_DIM ** -0.5)).astype(BF16)
    v_ref[...] = sec(3).astype(BF16)
    o_ref[...] = sec(4)
    g = jnp.dot(xn, wg_ref[...], preferred_element_type=F32) + gb_ref[...]
    lane = lax.broadcasted_iota(I32, g.shape, 1)
    logsig = jnp.minimum(g, 0.0) - jnp.log1p(jnp.exp(-jnp.abs(g)))
    gt_ref[...] = jnp.where(lane >= 2 * HEADS, logsig, g)


def _inproj(x2, norm_g, w_main, w_gate, gate_b):
    T = x2.shape[0]
    tm = TM_PROJ
    tok = lambda i: (i, 0)
    fix = lambda i: (0, 0)
    outs = (
        jax.ShapeDtypeStruct((T, POOL_WIDTH), F32),
        jax.ShapeDtypeStruct((T, MLSTM_WIDTH), BF16),
        jax.ShapeDtypeStruct((T, MLSTM_WIDTH), BF16),
        jax.ShapeDtypeStruct((T, MLSTM_WIDTH), BF16),
        jax.ShapeDtypeStruct((T, MLSTM_WIDTH), F32),
        jax.ShapeDtypeStruct((T, LANES), F32),
    )
    return pl.pallas_call(
        _inproj_kernel,
        grid=(T // tm,),
        in_specs=[
            pl.BlockSpec((tm, D_MODEL), tok),
            pl.BlockSpec((1, D_MODEL), fix),
            pl.BlockSpec((D_MODEL, 5 * 512), fix),
            pl.BlockSpec((D_MODEL, LANES), fix),
            pl.BlockSpec((1, LANES), fix),
        ],
        out_specs=[
            pl.BlockSpec((tm, 512), tok), pl.BlockSpec((tm, 512), tok), pl.BlockSpec((tm, 512), tok),
            pl.BlockSpec((tm, 512), tok), pl.BlockSpec((tm, 512), tok), pl.BlockSpec((tm, LANES), tok),
        ],
        out_shape=outs,
        compiler_params=_cparams(1),
        name="inproj",
    )(x2, norm_g, w_main, w_gate, gate_b)


def _mlstm_kernel(*refs, rev):
    if rev:
        (q_ref, k_ref, v_ref, gt_ref, tri_ref, hf_ref, o_ref, hg_ref, out_ref, c_sc, m_sc) = refs
    else:
        (q_ref, k_ref, v_ref, gt_ref, tri_ref, out_ref, c_sc, m_sc) = refs
    L = CHUNK

    @pl.when(pl.program_id(1) == 0)
    def _():
        c_sc[...] = jnp.zeros_like(c_sc)
        m_sc[...] = jnp.zeros_like(m_sc)

    g = gt_ref[...]
    tri = tri_ref[...]
    mask = tri > 0.0
    b_all = jnp.dot(tri, g, precision=HIGHEST, preferred_element_type=F32)
    ba = pltpu.roll(b_all, LANES - 2 * HEADS, axis=1)
    r_rows = (g - ba).T
    b_last = ba[0:1, :] if rev else ba[L - 1:L, :]
    m_prev = m_sc[...]
    logu = b_last - ba + g
    m_new = jnp.maximum(b_last + m_prev, jnp.max(logu, axis=0, keepdims=True))
    u_all = jnp.exp(logu - m_new)
    decay_all = jnp.exp(b_last + m_prev - m_new)
    inter_log_all = ba + m_prev
    m_sc[...] = m_new

    e0 = (lax.broadcasted_iota(I32, (L, HEAD_DIM), 1) == 0).astype(BF16)
    for h in range(HEADS):
        c = HEADS * rev + h
        hs = slice(h * HEAD_DIM, (h + 1) * HEAD_DIM)
        qh = q_ref[:, hs]
        kh = k_ref[:, hs]
        v_aug = jnp.concatenate([v_ref[:, hs], e0], axis=1)
        b_col = ba[:, c:c + 1]
        logw = jnp.where(mask, b_col + r_rows[c:c + 1, :], -jnp.inf)
        inter_log = inter_log_all[:, c:c + 1]
        m_row = jnp.maximum(inter_log, jnp.max(logw, axis=-1, keepdims=True))
        dmat = jnp.exp(logw - m_row)
        s = lax.dot_general(qh, kh, (((1,), (1,)), ((), ())), preferred_element_type=F32) * dmat
        intra = jnp.dot(s.astype(BF16), v_aug, preferred_element_type=F32)
        c_old = c_sc[h]
        cross = jnp.dot(qh, c_old.astype(BF16), preferred_element_type=F32)
        tot = intra + jnp.exp(inter_log - m_row) * cross
        den = tot[:, HEAD_DIM:HEAD_DIM + 1]
        hh = tot[:, :HEAD_DIM] / jnp.maximum(jnp.abs(den), jnp.exp(-m_row))

        ku = (kh.astype(F32) * u_all[:, c:c + 1]).astype(BF16)
        upd = lax.dot_general(ku, v_aug, (((0,), (0,)), ((), ())), preferred_element_type=F32)
        c_sc[h] = decay_all[:, c:c + 1] * c_old + upd

        if rev:
            hs_sum = hf_ref[:, hs] + hh
            hn = hs_sum * lax.rsqrt(jnp.mean(hs_sum * hs_sum, axis=-1, keepdims=True) + EPS)
            out_ref[:, hs] = (jax.nn.sigmoid(o_ref[:, hs]) * (hn * hg_ref[:, hs])).astype(out_ref.dtype)
        else:
            out_ref[:, hs] = hh


def _mlstm(q, k, v, gt, tri, B, S, rev, hf=None, o=None, head_g=None):
    T = B * S
    nc = S // CHUNK
    if rev:
        tok = lambda b, c: (b * nc + (nc - 1 - c), 0)
    else:
        tok = lambda b, c: (b * nc + c, 0)
    fix = lambda b, c: (0, 0)
    blk = lambda w: pl.BlockSpec((CHUNK, w), tok)
    in_specs = [blk(512), blk(512), blk(512), blk(LANES), pl.BlockSpec((CHUNK, CHUNK), fix)]
    args = [q, k, v, gt, tri]
    if rev:
        in_specs += [blk(512), blk(512), pl.BlockSpec((1, 512), fix)]
        args += [hf, o, head_g]
    return pl.pallas_call(
        functools.partial(_mlstm_kernel, rev=rev),
        grid=(B, nc),
        in_specs=in_specs,
        out_specs=blk(512),
        out_shape=jax.ShapeDtypeStruct((T, MLSTM_WIDTH), BF16 if rev else F32),
        scratch_shapes=[pltpu.VMEM((HEADS, HEAD_DIM, 2 * HEAD_DIM), F32), pltpu.VMEM((1, LANES), F32)],
        compiler_params=_cparams(2),
        name="mlstm_bwd" if rev else "mlstm_fwd",
    )(*args)


def _mix_kernel(x_ref, u_ref, up_ref, un_ref, m_ref, pw_ref, ps_ref, wo_ref, g2_ref, wr_ref, br_ref, su_ref,
                cin_ref, x1_ref, xn_ref, eid_ref, rank_ref, ew_ref, cnt_ref, ext_sc, run_sc, *, seq_len):
    tm = TM_MIX
    i = pl.program_id(0)
    tps = seq_len // tm
    it = i % tps

    @pl.when(i == 0)
    def _():
        run_sc[...] = cin_ref[...]

    ext_sc[0:POOL_HALO, :] = jnp.where(it == 0, 0.0, up_ref[...])
    ext_sc[POOL_HALO:POOL_HALO + tm, :] = u_ref[...]
    ext_sc[POOL_HALO + tm:, :] = jnp.where(it == tps - 1, 0.0, un_ref[...])
    t_seq = it * tm + lax.broadcasted_iota(I32, (tm, 1), 0)
    a_parts = []
    for gi, w in enumerate(POOL_WINDOWS):
        ls = slice(gi * POOL_GROUP, (gi + 1) * POOL_GROUP)
        acc = None
        for kk in range(-(w // 2), w // 2):
            piece = ext_sc[POOL_HALO + kk:POOL_HALO + kk + tm, ls]
            acc = piece if acc is None else acc + piece
        cnt = (jnp.minimum(t_seq + w // 2, seq_len) - jnp.maximum(t_seq - w // 2, 0)).astype(F32)
        d = acc / cnt - u_ref[:, ls]
        a_parts.append(jnp.dot(d.astype(BF16), pw_ref[gi], preferred_element_type=F32))
    a = (jnp.concatenate(a_parts, axis=1) * ps_ref[...]).astype(BF16)

    x1 = (x_ref[...]
          + jnp.dot(a, wo_ref[0:POOL_WIDTH, :], preferred_element_type=F32)
          + jnp.dot(m_ref[...], wo_ref[POOL_WIDTH:, :], preferred_element_type=F32))
    x1_ref[...] = x1
    xn = x1 * lax.rsqrt(jnp.mean(x1 * x1, axis=-1, keepdims=True) + EPS) * g2_ref[...]
    xn_ref[...] = xn

    lg = lax.dot_general(wr_ref[...], xn, (((1,), (1,)), ((), ())), precision=HIGHEST,
                         preferred_element_type=F32) + br_ref[:, 0:1]
    gl = lg[0:N_GROUPS, :]
    gmax = jnp.max(gl, axis=0, keepdims=True)
    giota = lax.broadcasted_iota(I32, gl.shape, 0).astype(F32)
    g_idx = jnp.min(jnp.where(gl == gmax, giota, float(N_GROUPS)), axis=0, keepdims=True)
    g_w = 1.0 / jnp.sum(jnp.exp(gl - gmax), axis=0, keepdims=True)
    el = lg[8:8 + EPG, :]
    for gg in range(1, N_GROUPS):
        el = jnp.where(g_idx == float(gg), lg[8 + gg * EPG:8 + (gg + 1) * EPG, :], el)
    eiota = lax.broadcasted_iota(I32, el.shape, 0).astype(F32)
    top1 = jnp.max(el, axis=0, keepdims=True)
    i1 = jnp.min(jnp.where(el == top1, eiota, float(EPG)), axis=0, keepdims=True)
    el2 = jnp.where(eiota == i1, -jnp.inf, el)
    top2 = jnp.max(el2, axis=0, keepdims=True)
    i2 = jnp.min(jnp.where(el2 == top2, eiota, float(EPG)), axis=0, keepdims=True)
    ex = jnp.exp(top2 - top1)
    w1 = g_w / (1.0 + ex)
    w2 = g_w * ex / (1.0 + ex)
    e1 = g_idx * float(EPG) + i1
    e2 = g_idx * float(EPG) + i2
    eid_ref[0:1, :] = e1.astype(I32)
    eid_ref[1:2, :] = e2.astype(I32)

    xiota = lax.broadcasted_iota(I32, (N_EXPERTS, tm), 0).astype(F32)
    oh1 = xiota == e1
    oh2 = xiota == e2
    oh = jnp.where(oh1 | oh2, 1.0, 0.0)
    before = jnp.dot(oh.astype(BF16), su_ref[...], preferred_element_type=F32)
    base = run_sc[:, 0:1] + before
    rank_ref[0:1, :] = jnp.sum(jnp.where(oh1, base, 0.0), axis=0, keepdims=True).astype(I32)
    rank_ref[1:2, :] = jnp.sum(jnp.where(oh2, base, 0.0), axis=0, keepdims=True).astype(I32)
    run_new = run_sc[...] + jnp.sum(oh, axis=1, keepdims=True)
    run_sc[...] = run_new
    cnt_ref[...] = run_new

    riota = lax.broadcasted_iota(I32, (LANES, tm), 0)
    wt = jnp.where(riota == 0, w1, jnp.where(riota == 1, w2, 0.0))
    ew_ref[...] = wt.T


def _mix(x2, u, m, pool_w, pool_scale, w_out, norm2_g, wr, br, su, counts_in, seq_len):
    T = x2.shape[0]
    tm = TM_MIX
    hb = tm // POOL_HALO
    nhb = T // POOL_HALO
    tok = lambda i: (i, 0)
    fix = lambda i: (0, 0)
    outs = (
        jax.ShapeDtypeStruct((T, D_MODEL), F32),
        jax.ShapeDtypeStruct((T, D_MODEL), F32),
        jax.ShapeDtypeStruct((2, T), I32),
        jax.ShapeDtypeStruct((2, T), I32),
        jax.ShapeDtypeStruct((T, LANES), F32),
        jax.ShapeDtypeStruct((N_EXPERTS, LANES), F32),
    )
    return pl.pallas_call(
        functools.partial(_mix_kernel, seq_len=seq_len),
        grid=(T // tm,),
        in_specs=[
            pl.BlockSpec((tm, D_MODEL), tok),
            pl.BlockSpec((tm, POOL_WIDTH), tok),
            pl.BlockSpec((POOL_HALO, POOL_WIDTH), lambda i: (jnp.maximum(i * hb - 1, 0), 0)),
            pl.BlockSpec((POOL_HALO, POOL_WIDTH), lambda i: (jnp.minimum((i + 1) * hb, nhb - 1), 0)),
            pl.BlockSpec((tm, MLSTM_WIDTH), tok),
            pl.BlockSpec((len(POOL_WINDOWS), POOL_GROUP, POOL_GROUP), lambda i: (0, 0, 0)),
            pl.BlockSpec((1, POOL_WIDTH), fix),
            pl.BlockSpec((D_MODEL, D_MODEL), fix),
            pl.BlockSpec((1, D_MODEL), fix),
            pl.BlockSpec((ROUTER_ROWS, D_MODEL), fix),
            pl.BlockSpec((ROUTER_ROWS, LANES), fix),
            pl.BlockSpec((tm, tm), fix),
            pl.BlockSpec((N_EXPERTS, LANES), fix),
        ],
        out_specs=[
            pl.BlockSpec((tm, D_MODEL), tok),
            pl.BlockSpec((tm, D_MODEL), tok),
            pl.BlockSpec((2, tm), lambda i: (0, i)),
            pl.BlockSpec((2, tm), lambda i: (0, i)),
            pl.BlockSpec((tm, LANES), tok),
            pl.BlockSpec((N_EXPERTS, LANES), fix),
        ],
        out_shape=outs,
        scratch_shapes=[pltpu.VMEM((tm + 2 * POOL_HALO, POOL_WIDTH), F32), pltpu.VMEM((N_EXPERTS, LANES), F32)],
        compiler_params=_cparams(1),
        name="mix_router",
    )(x2, u, u, u, m, pool_w, pool_scale, w_out, norm2_g, wr, br, su, counts_in)


def _row_copy(src, dst, sem):
    return pltpu.make_async_copy(src, dst, sem)


def _push_kernel(d0_ref, d1_ref, xn_ref, rows_in_ref, rows_ref, sem):
    del rows_in_ref
    tm = TM_ROWS

    def issue(t, carry):
        src = xn_ref.at[pl.ds(t, 1), :]
        _row_copy(src, rows_ref.at[pl.ds(d0_ref[0, 0, t], 1), :], sem).start()
        _row_copy(src, rows_ref.at[pl.ds(d1_ref[0, 0, t], 1), :], sem).start()
        return carry

    lax.fori_loop(0, tm, issue, 0)

    def drain(t, carry):
        src = xn_ref.at[pl.ds(t, 1), :]
        _row_copy(src, rows_ref.at[pl.ds(d0_ref[0, 0, t], 1), :], sem).wait()
        _row_copy(src, rows_ref.at[pl.ds(d1_ref[0, 0, t], 1), :], sem).wait()
        return carry

    lax.fori_loop(0, tm, drain, 0)


def _push(dest0, dest1, xn, rows_init):
    T = xn.shape[0]
    tm = TM_ROWS
    smem_blk = pl.BlockSpec((1, 1, tm), lambda i: (i, 0, 0), memory_space=pltpu.SMEM)
    return pl.pallas_call(
        _push_kernel,
        grid=(T // tm,),
        in_specs=[smem_blk, smem_blk,
                  pl.BlockSpec((tm, D_MODEL), lambda i: (i, 0)),
                  pl.BlockSpec(memory_space=pl.ANY)],
        out_specs=pl.BlockSpec(memory_space=pl.ANY),
        out_shape=jax.ShapeDtypeStruct(rows_init.shape, rows_init.dtype),
        scratch_shapes=[pltpu.SemaphoreType.DMA(())],
        input_output_aliases={3: 0},
        compiler_params=_cparams(1),
        name="row_push",
    )(dest0, dest1, xn, rows_init)


def _expert_kernel(be_ref, nv_ref, x_ref, wg_ref, wu_ref, wd_ref, y_ref):
    del be_ref

    @pl.when(pl.program_id(0) < nv_ref[0])
    def _():
        xb = x_ref[...].astype(BF16)
        gate = jnp.dot(xb, wg_ref[0], preferred_element_type=F32)
        up = jnp.dot(xb, wu_ref[0], preferred_element_type=F32)
        hb = (gate * jax.nn.sigmoid(gate) * up).astype(BF16)
        y_ref[...] = jnp.dot(hb, wd_ref[0], preferred_element_type=F32)

    @pl.when(pl.program_id(0) >= nv_ref[0])
    def _():
        y_ref[...] = jnp.zeros_like(y_ref)


def _experts(block_e, n_valid, rows, w_gate, w_up, w_down):
    P = rows.shape[0]
    bm = BM_EXPERT
    grid_spec = pltpu.PrefetchScalarGridSpec(
        num_scalar_prefetch=2,
        grid=(P // bm,),
        in_specs=[
            pl.BlockSpec((bm, D_MODEL), lambda i, be, nv: (i, 0)),
            pl.BlockSpec((1, D_MODEL, D_EXPERT), lambda i, be, nv: (be[i], 0, 0)),
            pl.BlockSpec((1, D_MODEL, D_EXPERT), lambda i, be, nv: (be[i], 0, 0)),
            pl.BlockSpec((1, D_EXPERT, D_MODEL), lambda i, be, nv: (be[i], 0, 0)),
        ],
        out_specs=pl.BlockSpec((bm, D_MODEL), lambda i, be, nv: (i, 0)),
    )
    return pl.pallas_call(
        _expert_kernel,
        grid_spec=grid_spec,
        out_shape=jax.ShapeDtypeStruct((P, D_MODEL), F32),
        compiler_params=_cparams(1),
        name="experts",
    )(block_e, n_valid, rows, w_gate, w_up, w_down)


def _combine_kernel(d0_ref, d1_ref, x1_ref, ew_ref, gf_ref, y_ref, out_ref, buf, sem):
    tm = TM_ROWS

    def issue(t, carry):
        _row_copy(y_ref.at[pl.ds(d0_ref[0, 0, t], 1), :], buf.at[0, pl.ds(t, 1), :], sem).start()
        _row_copy(y_ref.at[pl.ds(d1_ref[0, 0, t], 1), :], buf.at[1, pl.ds(t, 1), :], sem).start()
        return carry

    lax.fori_loop(0, tm, issue, 0)

    def drain(t, carry):
        _row_copy(y_ref.at[pl.ds(d0_ref[0, 0, t], 1), :], buf.at[0, pl.ds(t, 1), :], sem).wait()
        _row_copy(y_ref.at[pl.ds(d1_ref[0, 0, t], 1), :], buf.at[1, pl.ds(t, 1), :], sem).wait()
        return carry

    lax.fori_loop(0, tm, drain, 0)

    ew = ew_ref[...]
    x = x1_ref[...] + buf[0] * ew[:, 0:1] + buf[1] * ew[:, 1:2]
    out_ref[...] = x * lax.rsqrt(jnp.mean(x * x, axis=-1, keepdims=True) + EPS) * gf_ref[...]


def _combine(dest0, dest1, x1, ew, final_g, yrows):
    T = x1.shape[0]
    tm = TM_ROWS
    smem_blk = pl.BlockSpec((1, 1, tm), lambda i: (i, 0, 0), memory_space=pltpu.SMEM)
    tok = lambda i: (i, 0)
    return pl.pallas_call(
        _combine_kernel,
        grid=(T // tm,),
        in_specs=[smem_blk, smem_blk,
                  pl.BlockSpec((tm, D_MODEL), tok),
                  pl.BlockSpec((tm, LANES), tok),
                  pl.BlockSpec((1, D_MODEL), lambda i: (0, 0)),
                  pl.BlockSpec(memory_space=pl.ANY)],
        out_specs=pl.BlockSpec((tm, D_MODEL), tok),
        out_shape=jax.ShapeDtypeStruct((T, D_MODEL), F32),
        scratch_shapes=[pltpu.VMEM((2, tm, D_MODEL), F32), pltpu.SemaphoreType.DMA(())],
        compiler_params=_cparams(1),
        name="combine",
    )(dest0, dest1, x1, ew, final_g, yrows)


def _prepare_params(norm1_g, w_in, pool_w, pool_scale, gate_b, head_norm_g, w_out, norm2_g,
                    router_group_w, router_group_b, router_expert_w, router_expert_b,
                    expert_w_gate, expert_w_up, expert_w_down, final_norm_g):
    w = w_in[0]
    n_main = 5 * 512
    w_main = w[:, :n_main].astype(BF16)
    w_gate = jnp.pad(w[:, n_main:], ((0, 0), (0, LANES - N_GATES))).astype(BF16)
    gb = jnp.pad(gate_b[0], (0, LANES - N_GATES)).reshape(1, LANES)
    wr = jnp.zeros((ROUTER_ROWS, D_MODEL), F32)
    wr = wr.at[0:N_GROUPS].set(router_group_w[0].T)
    wr = wr.at[8:].set(router_expert_w[0].reshape(D_MODEL, N_EXPERTS).T)
    br = jnp.zeros((ROUTER_ROWS,), F32)
    br = br.at[0:N_GROUPS].set(router_group_b[0])
    br = br.at[8:].set(router_expert_b[0].reshape(N_EXPERTS))
    br = jnp.broadcast_to(br[:, None], (ROUTER_ROWS, LANES))
    idx = jnp.arange(CHUNK)
    tri_f = (idx[None, :] <= idx[:, None]).astype(F32)
    tri_b = (idx[None, :] >= idx[:, None]).astype(F32)
    tidx = jnp.arange(TM_MIX)
    su = (tidx[:, None] < tidx[None, :]).astype(BF16)
    return dict(
        norm1_g=norm1_g[0].reshape(1, D_MODEL), w_main=w_main, w_gate=w_gate, gate_b=gb,
        pool_w=pool_w[0].astype(BF16), pool_scale=pool_scale[0].reshape(1, POOL_WIDTH),
        head_g=head_norm_g[0].reshape(1, MLSTM_WIDTH), w_out=w_out[0].astype(BF16),
        norm2_g=norm2_g[0].reshape(1, D_MODEL), wr=wr, br=br, tri_f=tri_f, tri_b=tri_b, su=su,
        e_gate=expert_w_gate[0].astype(BF16), e_up=expert_w_up[0].astype(BF16),
        e_down=expert_w_down[0].astype(BF16), final_g=final_norm_g.reshape(1, D_MODEL),
    )


def _encoder(x, p):
    B, S, _ = x.shape
    T = B * S
    x2 = x.reshape(T, D_MODEL)
    u, q, k, v, o, gt = _inproj(x2, p["norm1_g"], p["w_main"], p["w_gate"], p["gate_b"])
    hf = _mlstm(q, k, v, gt, p["tri_f"], B, S, rev=False)
    m = _mlstm(q, k, v, gt, p["tri_b"], B, S, rev=True, hf=hf, o=o, head_g=p["head_g"])
    counts0 = jnp.zeros((N_EXPERTS, LANES), F32)
    x1, xn, eid, rank, ew, cnt = _mix(x2, u, m, p["pool_w"], p["pool_scale"], p["w_out"], p["norm2_g"],
                                      p["wr"], p["br"], p["su"], counts0, S)

    bm = BM_EXPERT
    n_assign = 2 * T
    nb = n_assign // bm + N_EXPERTS
    counts = cnt[:, 0].astype(I32)
    pcounts = ((counts + bm - 1) // bm) * bm
    pends = jnp.cumsum(pcounts)
    pstarts = pends - pcounts
    dest = pstarts[eid] + rank
    block_e = jnp.clip(jnp.searchsorted(pends, jnp.arange(nb, dtype=I32) * bm, side="right"),
                       0, N_EXPERTS - 1).astype(I32)
    n_valid = (pends[-1:] // bm).astype(I32)
    nt = T // TM_ROWS
    dest0 = dest[0].reshape(nt, 1, TM_ROWS)
    dest1 = dest[1].reshape(nt, 1, TM_ROWS)

    rows = _push(dest0, dest1, xn, jnp.zeros((nb * bm, D_MODEL), F32))
    yrows = _experts(block_e, n_valid, rows, p["e_gate"], p["e_up"], p["e_down"])
    y = _combine(dest0, dest1, x1, ew, p["final_g"], yrows)
    return y.reshape(B, S, D_MODEL)


def kernel(x_prompt, x_sample, norm1_g, w_in, pool_w, pool_scale, gate_b, head_norm_g, w_out, norm2_g,
           router_group_w, router_group_b, router_expert_w, router_expert_b,
           expert_w_gate, expert_w_up, expert_w_down, final_norm_g):
    p = _prepare_params(norm1_g, w_in, pool_w, pool_scale, gate_b, head_norm_g, w_out, norm2_g,
                        router_group_w, router_group_b, router_expert_w, router_expert_b,
                        expert_w_gate, expert_w_up, expert_w_down, final_norm_g)
    return (_encoder(x_prompt, p), _encoder(x_sample, p))
```

```python
import functools

import jax
import jax.numpy as jnp
from jax import lax
from jax.experimental import pallas as pl
from jax.experimental.pallas import tpu as pltpu

F32 = jnp.float32
BF16 = jnp.bfloat16
I32 = jnp.int32

D_MODEL = 1024
POOL_WIDTH = 512
POOL_WINDOWS = (2, 4, 8, 16)
POOL_GROUP = 128
POOL_HALO = 8
MLSTM_WIDTH = 512
HEADS = 4
HEAD_DIM = 128
CHUNK = 128
N_GATES = 16
N_GROUPS = 4
EPG = 8
N_EXPERTS = 32
D_EXPERT = 512
EPS = 1e-6
LANES = 128
ROUTER_ROWS = 8 + N_EXPERTS

TM_PROJ = 512
TM_MIX = 512
TM_ROWS = 256
ROW_SUB = D_MODEL // LANES
BM_EXPERT = 256
VMEM_LIMIT = 56 * 1024 * 1024

HIGHEST = lax.Precision.HIGHEST


def _cparams(n_axes):
    return pltpu.CompilerParams(dimension_semantics=("arbitrary",) * n_axes,
                                vmem_limit_bytes=VMEM_LIMIT)


def _store_row_tiles(ref, val):
    rows = val.shape[0]
    for j in range(ROW_SUB):
        ref[pl.ds(j, rows, stride=ROW_SUB), :] = val[:, j * LANES:(j + 1) * LANES]


def _load_row_tiles(ref):
    rows = ref.shape[0] // ROW_SUB
    return jnp.concatenate([ref[pl.ds(j, rows, stride=ROW_SUB), :] for j in range(ROW_SUB)], axis=1)


def _row_tile(ref, r):
    return ref.at[pl.ds(pl.multiple_of(r * ROW_SUB, ROW_SUB), ROW_SUB), :]


def _inproj_kernel(x_ref, g_ref, w_ref, wg_ref, gb_ref, u_ref, q_ref, k_ref, v_ref, o_ref, gt_ref):
    x = x_ref[...]
    inv = lax.rsqrt(jnp.mean(x * x, axis=-1, keepdims=True) + EPS)
    xn = (x * inv * g_ref[...]).astype(BF16)

    def sec(i):
        return jnp.dot(xn, w_ref[:, i * 512:(i + 1) * 512], preferred_element_type=F32)

    u_ref[...] = sec(0)
    q_ref[...] = sec(1).astype(BF16)
    k_ref[...] = (sec(2) * (HEAD_DIM ** -0.5)).astype(BF16)
    v_ref[...] = sec(3).astype(BF16)
    o_ref[...] = sec(4)
    g = jnp.dot(xn, wg_ref[...], preferred_element_type=F32) + gb_ref[...]
    lane = lax.broadcasted_iota(I32, g.shape, 1)
    logsig = jnp.minimum(g, 0.0) - jnp.log1p(jnp.exp(-jnp.abs(g)))
    gt_ref[...] = jnp.where(lane >= 2 * HEADS, logsig, g)


def _inproj(x2, norm_g, w_main, w_gate, gate_b):
    T = x2.shape[0]
    tm = TM_PROJ
    tok = lambda i: (i, 0)
    fix = lambda i: (0, 0)
    outs = (
        jax.ShapeDtypeStruct((T, POOL_WIDTH), F32),
        jax.ShapeDtypeStruct((T, MLSTM_WIDTH), BF16),
        jax.ShapeDtypeStruct((T, MLSTM_WIDTH), BF16),
        jax.ShapeDtypeStruct((T, MLSTM_WIDTH), BF16),
        jax.ShapeDtypeStruct((T, MLSTM_WIDTH), F32),
        jax.ShapeDtypeStruct((T, LANES), F32),
    )
    return pl.pallas_call(
        _inproj_kernel,
        grid=(T // tm,),
        in_specs=[
            pl.BlockSpec((tm, D_MODEL), tok),
            pl.BlockSpec((1, D_MODEL), fix),
            pl.BlockSpec((D_MODEL, 5 * 512), fix),
            pl.BlockSpec((D_MODEL, LANES), fix),
            pl.BlockSpec((1, LANES), fix),
        ],
        out_specs=[
            pl.BlockSpec((tm, 512), tok), pl.BlockSpec((tm, 512), tok), pl.BlockSpec((tm, 512), tok),
            pl.BlockSpec((tm, 512), tok), pl.BlockSpec((tm, 512), tok), pl.BlockSpec((tm, LANES), tok),
        ],
        out_shape=outs,
        compiler_params=_cparams(1),
        name="inproj",
    )(x2, norm_g, w_main, w_gate, gate_b)


def _mlstm_kernel(*refs, rev):
    if rev:
        (q_ref, k_ref, v_ref, gt_ref, tri_ref, hf_ref, o_ref, hg_ref, out_ref, c_sc, m_sc) = refs
    else:
        (q_ref, k_ref, v_ref, gt_ref, tri_ref, out_ref, c_sc, m_sc) = refs
    L = CHUNK

    @pl.when(pl.program_id(1) == 0)
    def _():
        c_sc[...] = jnp.zeros_like(c_sc)
        m_sc[...] = jnp.zeros_like(m_sc)

    g = gt_ref[...]
    tri = tri_ref[...]
    mask = tri > 0.0
    b_all = jnp.dot(tri, g, precision=HIGHEST, preferred_element_type=F32)
    ba = pltpu.roll(b_all, LANES - 2 * HEADS, axis=1)
    r_rows = (g - ba).T
    b_last = ba[0:1, :] if rev else ba[L - 1:L, :]
    m_prev = m_sc[...]
    logu = b_last - ba + g
    m_new = jnp.maximum(b_last + m_prev, jnp.max(logu, axis=0, keepdims=True))
    u_all = jnp.exp(logu - m_new)
    decay_all = jnp.exp(b_last + m_prev - m_new)
    inter_log_all = ba + m_prev
    m_sc[...] = m_new

    e0 = (lax.broadcasted_iota(I32, (L, HEAD_DIM), 1) == 0).astype(BF16)
    for h in range(HEADS):
        c = HEADS * rev + h
        hs = slice(h * HEAD_DIM, (h + 1) * HEAD_DIM)
        qh = q_ref[:, hs]
        kh = k_ref[:, hs]
        v_aug = jnp.concatenate([v_ref[:, hs], e0], axis=1)
        b_col = ba[:, c:c + 1]
        logw = jnp.where(mask, b_col + r_rows[c:c + 1, :], -jnp.inf)
        inter_log = inter_log_all[:, c:c + 1]
        m_row = jnp.maximum(inter_log, jnp.max(logw, axis=-1, keepdims=True))
        dmat = jnp.exp(logw - m_row)
        s = lax.dot_general(qh, kh, (((1,), (1,)), ((), ())), preferred_element_type=F32) * dmat
        intra = jnp.dot(s.astype(BF16), v_aug, preferred_element_type=F32)
        c_old = c_sc[h]
        cross = jnp.dot(qh, c_old.astype(BF16), preferred_element_type=F32)
        tot = intra + jnp.exp(inter_log - m_row) * cross
        den = tot[:, HEAD_DIM:HEAD_DIM + 1]
        hh = tot[:, :HEAD_DIM] / jnp.maximum(jnp.abs(den), jnp.exp(-m_row))

        ku = (kh.astype(F32) * u_all[:, c:c + 1]).astype(BF16)
        upd = lax.dot_general(ku, v_aug, (((0,), (0,)), ((), ())), preferred_element_type=F32)
        c_sc[h] = decay_all[:, c:c + 1] * c_old + upd

        if rev:
            hs_sum = hf_ref[:, hs] + hh
            hn = hs_sum * lax.rsqrt(jnp.mean(hs_sum * hs_sum, axis=-1, keepdims=True) + EPS)
            out_ref[:, hs] = (jax.nn.sigmoid(o_ref[:, hs]) * (hn * hg_ref[:, hs])).astype(out_ref.dtype)
        else:
            out_ref[:, hs] = hh


def _mlstm(q, k, v, gt, tri, B, S, rev, hf=None, o=None, head_g=None):
    T = B * S
    nc = S // CHUNK
    if rev:
        tok = lambda b, c: (b * nc + (nc - 1 - c), 0)
    else:
        tok = lambda b, c: (b * nc + c, 0)
    fix = lambda b, c: (0, 0)
    blk = lambda w: pl.BlockSpec((CHUNK, w), tok)
    in_specs = [blk(512), blk(512), blk(512), blk(LANES), pl.BlockSpec((CHUNK, CHUNK), fix)]
    args = [q, k, v, gt, tri]
    if rev:
        in_specs += [blk(512), blk(512), pl.BlockSpec((1, 512), fix)]
        args += [hf, o, head_g]
    return pl.pallas_call(
        functools.partial(_mlstm_kernel, rev=rev),
        grid=(B, nc),
        in_specs=in_specs,
        out_specs=blk(512),
        out_shape=jax.ShapeDtypeStruct((T, MLSTM_WIDTH), BF16 if rev else F32),
        scratch_shapes=[pltpu.VMEM((HEADS, HEAD_DIM, 2 * HEAD_DIM), F32), pltpu.VMEM((1, LANES), F32)],
        compiler_params=_cparams(2),
        name="mlstm_bwd" if rev else "mlstm_fwd",
    )(*args)


def _mix_kernel(x_ref, u_ref, up_ref, un_ref, m_ref, pw_ref, ps_ref, wo_ref, g2_ref, wr_ref, br_ref, su_ref,
                cin_ref, x1_ref, xn_ref, eid_ref, rank_ref, ew_ref, cnt_ref, ext_sc, run_sc, *, seq_len):
    tm = TM_MIX
    i = pl.program_id(0)
    tps = seq_len // tm
    it = i % tps

    @pl.when(i == 0)
    def _():
        run_sc[...] = cin_ref[...]

    ext_sc[0:POOL_HALO, :] = jnp.where(it == 0, 0.0, up_ref[...])
    ext_sc[POOL_HALO:POOL_HALO + tm, :] = u_ref[...]
    ext_sc[POOL_HALO + tm:, :] = jnp.where(it == tps - 1, 0.0, un_ref[...])
    t_seq = it * tm + lax.broadcasted_iota(I32, (tm, 1), 0)
    a_parts = []
    for gi, w in enumerate(POOL_WINDOWS):
        ls = slice(gi * POOL_GROUP, (gi + 1) * POOL_GROUP)
        acc = None
        for kk in range(-(w // 2), w // 2):
            piece = ext_sc[POOL_HALO + kk:POOL_HALO + kk + tm, ls]
            acc = piece if acc is None else acc + piece
        cnt = (jnp.minimum(t_seq + w // 2, seq_len) - jnp.maximum(t_seq - w // 2, 0)).astype(F32)
        d = acc / cnt - u_ref[:, ls]
        a_parts.append(jnp.dot(d.astype(BF16), pw_ref[gi], preferred_element_type=F32))
    a = (jnp.concatenate(a_parts, axis=1) * ps_ref[...]).astype(BF16)

    x1 = (x_ref[...]
          + jnp.dot(a, wo_ref[0:POOL_WIDTH, :], preferred_element_type=F32)
          + jnp.dot(m_ref[...], wo_ref[POOL_WIDTH:, :], preferred_element_type=F32))
    x1_ref[...] = x1
    xn = x1 * lax.rsqrt(jnp.mean(x1 * x1, axis=-1, keepdims=True) + EPS) * g2_ref[...]
    _store_row_tiles(xn_ref, xn)

    lg = lax.dot_general(wr_ref[...], xn, (((1,), (1,)), ((), ())), precision=HIGHEST,
                         preferred_element_type=F32) + br_ref[:, 0:1]
    gl = lg[0:N_GROUPS, :]
    gmax = jnp.max(gl, axis=0, keepdims=True)
    giota = lax.broadcasted_iota(I32, gl.shape, 0).astype(F32)
    g_idx = jnp.min(jnp.where(gl == gmax, giota, float(N_GROUPS)), axis=0, keepdims=True)
    g_w = 1.0 / jnp.sum(jnp.exp(gl - gmax), axis=0, keepdims=True)
    el = lg[8:8 + EPG, :]
    for gg in range(1, N_GROUPS):
        el = jnp.where(g_idx == float(gg), lg[8 + gg * EPG:8 + (gg + 1) * EPG, :], el)
    eiota = lax.broadcasted_iota(I32, el.shape, 0).astype(F32)
    top1 = jnp.max(el, axis=0, keepdims=True)
    i1 = jnp.min(jnp.where(el == top1, eiota, float(EPG)), axis=0, keepdims=True)
    el2 = jnp.where(eiota == i1, -jnp.inf, el)
    top2 = jnp.max(el2, axis=0, keepdims=True)
    i2 = jnp.min(jnp.where(el2 == top2, eiota, float(EPG)), axis=0, keepdims=True)
    ex = jnp.exp(top2 - top1)
    w1 = g_w / (1.0 + ex)
    w2 = g_w * ex / (1.0 + ex)
    e1 = g_idx * float(EPG) + i1
    e2 = g_idx * float(EPG) + i2
    eid_ref[0:1, :] = e1.astype(I32)
    eid_ref[1:2, :] = e2.astype(I32)

    xiota = lax.broadcasted_iota(I32, (N_EXPERTS, tm), 0).astype(F32)
    oh1 = xiota == e1
    oh2 = xiota == e2
    oh = jnp.where(oh1 | oh2, 1.0, 0.0)
    before = jnp.dot(oh.astype(BF16), su_ref[...], preferred_element_type=F32)
    base = run_sc[:, 0:1] + before
    rank_ref[0:1, :] = jnp.sum(jnp.where(oh1, base, 0.0), axis=0, keepdims=True).astype(I32)
    rank_ref[1:2, :] = jnp.sum(jnp.where(oh2, base, 0.0), axis=0, keepdims=True).astype(I32)
    run_new = run_sc[...] + jnp.sum(oh, axis=1, keepdims=True)
    run_sc[...] = run_new
    cnt_ref[...] = run_new

    riota = lax.broadcasted_iota(I32, (LANES, tm), 0)
    wt = jnp.where(riota == 0, w1, jnp.where(riota == 1, w2, 0.0))
    ew_ref[...] = wt.T


def _mix(x2, u, m, pool_w, pool_scale, w_out, norm2_g, wr, br, su, counts_in, seq_len):
    T = x2.shape[0]
    tm = TM_MIX
    hb = tm // POOL_HALO
    nhb = T // POOL_HALO
    tok = lambda i: (i, 0)
    fix = lambda i: (0, 0)
    outs = (
        jax.ShapeDtypeStruct((T, D_MODEL), F32),
        jax.ShapeDtypeStruct((T * ROW_SUB, LANES), F32),
        jax.ShapeDtypeStruct((2, T), I32),
        jax.ShapeDtypeStruct((2, T), I32),
        jax.ShapeDtypeStruct((T, LANES), F32),
        jax.ShapeDtypeStruct((N_EXPERTS, LANES), F32),
    )
    return pl.pallas_call(
        functools.partial(_mix_kernel, seq_len=seq_len),
        grid=(T // tm,),
        in_specs=[
            pl.BlockSpec((tm, D_MODEL), tok),
            pl.BlockSpec((tm, POOL_WIDTH), tok),
            pl.BlockSpec((POOL_HALO, POOL_WIDTH), lambda i: (jnp.maximum(i * hb - 1, 0), 0)),
            pl.BlockSpec((POOL_HALO, POOL_WIDTH), lambda i: (jnp.minimum((i + 1) * hb, nhb - 1), 0)),
            pl.BlockSpec((tm, MLSTM_WIDTH), tok),
            pl.BlockSpec((len(POOL_WINDOWS), POOL_GROUP, POOL_GROUP), lambda i: (0, 0, 0)),
            pl.BlockSpec((1, POOL_WIDTH), fix),
            pl.BlockSpec((D_MODEL, D_MODEL), fix),
            pl.BlockSpec((1, D_MODEL), fix),
            pl.BlockSpec((ROUTER_ROWS, D_MODEL), fix),
            pl.BlockSpec((ROUTER_ROWS, LANES), fix),
            pl.BlockSpec((tm, tm), fix),
            pl.BlockSpec((N_EXPERTS, LANES), fix),
        ],
        out_specs=[
            pl.BlockSpec((tm, D_MODEL), tok),
            pl.BlockSpec((tm * ROW_SUB, LANES), tok),
            pl.BlockSpec((2, tm), lambda i: (0, i)),
            pl.BlockSpec((2, tm), lambda i: (0, i)),
            pl.BlockSpec((tm, LANES), tok),
            pl.BlockSpec((N_EXPERTS, LANES), fix),
        ],
        out_shape=outs,
        scratch_shapes=[pltpu.VMEM((tm + 2 * POOL_HALO, POOL_WIDTH), F32), pltpu.VMEM((N_EXPERTS, LANES), F32)],
        compiler_params=_cparams(1),
        name="mix_router",
    )(x2, u, u, u, m, pool_w, pool_scale, w_out, norm2_g, wr, br, su, counts_in)


def _row_copy(src, dst, sem):
    return pltpu.make_async_copy(src, dst, sem)


def _push_kernel(zs_ref, d0_ref, d1_ref, xn_ref, rows_ref, zero_sc, sem, zsem):
    tm = TM_ROWS

    @pl.when(pl.program_id(0) == 0)
    def _():
        zero_sc[...] = jnp.zeros_like(zero_sc)

        def fill(e):
            start = pl.multiple_of(jnp.maximum(zs_ref[e], 0) * ROW_SUB, ROW_SUB)
            return _row_copy(zero_sc, rows_ref.at[pl.ds(start, BM_EXPERT * ROW_SUB), :], zsem)

        for e in range(2 * N_EXPERTS):
            @pl.when(zs_ref[e] >= 0)
            def _():
                fill(e).start()
        for e in range(2 * N_EXPERTS):
            @pl.when(zs_ref[e] >= 0)
            def _():
                fill(e).wait()

    def issue(t, carry):
        src = _row_tile(xn_ref, t)
        _row_copy(src, _row_tile(rows_ref, d0_ref[0, 0, t]), sem).start(priority=0)
        _row_copy(src, _row_tile(rows_ref, d1_ref[0, 0, t]), sem).start(priority=1)
        return carry

    lax.fori_loop(0, tm, issue, 0, unroll=4)

    def drain(t, carry):
        src = _row_tile(xn_ref, t)
        _row_copy(src, _row_tile(rows_ref, d0_ref[0, 0, t]), sem).wait()
        _row_copy(src, _row_tile(rows_ref, d1_ref[0, 0, t]), sem).wait()
        return carry

    lax.fori_loop(0, tm, drain, 0, unroll=4)


def _push(zero_start, dest0, dest1, xn, n_rows):
    T = xn.shape[0] // ROW_SUB
    tm = TM_ROWS
    smem_blk = pl.BlockSpec((1, 1, tm), lambda i, zs: (i, 0, 0), memory_space=pltpu.SMEM)
    grid_spec = pltpu.PrefetchScalarGridSpec(
        num_scalar_prefetch=1,
        grid=(T // tm,),
        in_specs=[smem_blk, smem_blk, pl.BlockSpec((tm * ROW_SUB, LANES), lambda i, zs: (i, 0))],
        out_specs=pl.BlockSpec(memory_space=pl.ANY),
        scratch_shapes=[pltpu.VMEM((BM_EXPERT * ROW_SUB, LANES), F32), pltpu.SemaphoreType.DMA(()),
                        pltpu.SemaphoreType.DMA(())],
    )
    return pl.pallas_call(
        _push_kernel,
        grid_spec=grid_spec,
        out_shape=jax.ShapeDtypeStruct((n_rows * ROW_SUB, LANES), F32),
        compiler_params=_cparams(1),
        name="row_push",
    )(zero_start, dest0, dest1, xn)


def _expert_kernel(be_ref, nv_ref, x_ref, wg_ref, wu_ref, wd_ref, y_ref, wg_sc, wu_sc, wd_sc):
    i = pl.program_id(0)
    valid = i < nv_ref[0]
    new_expert = (i == 0) | (be_ref[i] != be_ref[jnp.maximum(i - 1, 0)])

    @pl.when(valid & new_expert)
    def _():
        wg_sc[...] = wg_ref[0].astype(BF16)
        wu_sc[...] = wu_ref[0].astype(BF16)
        wd_sc[...] = wd_ref[0].astype(BF16)

    @pl.when(valid)
    def _():
        xb = _load_row_tiles(x_ref).astype(BF16)
        gate = jnp.dot(xb, wg_sc[...], preferred_element_type=F32)
        up = jnp.dot(xb, wu_sc[...], preferred_element_type=F32)
        hb = (gate * jax.nn.sigmoid(gate) * up).astype(BF16)
        _store_row_tiles(y_ref, jnp.dot(hb, wd_sc[...], preferred_element_type=F32))

    @pl.when(jnp.logical_not(valid))
    def _():
        y_ref[...] = jnp.zeros_like(y_ref)


def _experts(block_e, n_valid, rows, w_gate, w_up, w_down):
    P = rows.shape[0] // ROW_SUB
    bm = BM_EXPERT
    row_blk = lambda i, be, nv: (jnp.minimum(i, nv[0] - 1), 0)
    w_blk = lambda i, be, nv: (be[jnp.minimum(i, nv[0] - 1)], 0, 0)
    grid_spec = pltpu.PrefetchScalarGridSpec(
        num_scalar_prefetch=2,
        grid=(P // bm,),
        in_specs=[
            pl.BlockSpec((bm * ROW_SUB, LANES), row_blk),
            pl.BlockSpec((1, D_MODEL, D_EXPERT), w_blk),
            pl.BlockSpec((1, D_MODEL, D_EXPERT), w_blk),
            pl.BlockSpec((1, D_EXPERT, D_MODEL), w_blk),
        ],
        out_specs=pl.BlockSpec((bm * ROW_SUB, LANES), lambda i, be, nv: (i, 0)),
        scratch_shapes=[pltpu.VMEM((D_MODEL, D_EXPERT), BF16), pltpu.VMEM((D_MODEL, D_EXPERT), BF16),
                        pltpu.VMEM((D_EXPERT, D_MODEL), BF16)],
    )
    return pl.pallas_call(
        _expert_kernel,
        grid_spec=grid_spec,
        out_shape=jax.ShapeDtypeStruct((P * ROW_SUB, LANES), F32),
        compiler_params=_cparams(1),
        name="experts",
    )(block_e, n_valid, rows, w_gate, w_up, w_down)


def _combine_kernel(d0_ref, d1_ref, x1_ref, ew_ref, gf_ref, y_ref, out_ref, buf, sem):
    tm = TM_ROWS

    def issue(t, carry):
        _row_copy(_row_tile(y_ref, d0_ref[0, 0, t]), _row_tile(buf.at[0], t), sem).start(priority=0)
        _row_copy(_row_tile(y_ref, d1_ref[0, 0, t]), _row_tile(buf.at[1], t), sem).start(priority=1)
        return carry

    lax.fori_loop(0, tm, issue, 0, unroll=4)

    def drain(t, carry):
        _row_copy(_row_tile(y_ref, d0_ref[0, 0, t]), _row_tile(buf.at[0], t), sem).wait()
        _row_copy(_row_tile(y_ref, d1_ref[0, 0, t]), _row_tile(buf.at[1], t), sem).wait()
        return carry

    lax.fori_loop(0, tm, drain, 0, unroll=4)

    ew = ew_ref[...]
    x = x1_ref[...] + _load_row_tiles(buf.at[0]) * ew[:, 0:1] + _load_row_tiles(buf.at[1]) * ew[:, 1:2]
    out_ref[...] = x * lax.rsqrt(jnp.mean(x * x, axis=-1, keepdims=True) + EPS) * gf_ref[...]


def _combine(dest0, dest1, x1, ew, final_g, yrows):
    T = x1.shape[0]
    tm = TM_ROWS
    smem_blk = pl.BlockSpec((1, 1, tm), lambda i: (i, 0, 0), memory_space=pltpu.SMEM)
    tok = lambda i: (i, 0)
    return pl.pallas_call(
        _combine_kernel,
        grid=(T // tm,),
        in_specs=[smem_blk, smem_blk,
                  pl.BlockSpec((tm, D_MODEL), tok),
                  pl.BlockSpec((tm, LANES), tok),
                  pl.BlockSpec((1, D_MODEL), lambda i: (0, 0)),
                  pl.BlockSpec(memory_space=pl.ANY)],
        out_specs=pl.BlockSpec((tm, D_MODEL), tok),
        out_shape=jax.ShapeDtypeStruct((T, D_MODEL), F32),
        scratch_shapes=[pltpu.VMEM((2, tm * ROW_SUB, LANES), F32), pltpu.SemaphoreType.DMA(())],
        compiler_params=_cparams(1),
        name="combine",
    )(dest0, dest1, x1, ew, final_g, yrows)


def _prepare_params(norm1_g, w_in, pool_w, pool_scale, gate_b, head_norm_g, w_out, norm2_g,
                    router_group_w, router_group_b, router_expert_w, router_expert_b,
                    expert_w_gate, expert_w_up, expert_w_down, final_norm_g):
    w = w_in[0]
    n_main = 5 * 512
    w_main = w[:, :n_main].astype(BF16)
    w_gate = jnp.pad(w[:, n_main:], ((0, 0), (0, LANES - N_GATES))).astype(BF16)
    gb = jnp.pad(gate_b[0], (0, LANES - N_GATES)).reshape(1, LANES)
    wr = jnp.zeros((ROUTER_ROWS, D_MODEL), F32)
    wr = wr.at[0:N_GROUPS].set(router_group_w[0].T)
    wr = wr.at[8:].set(router_expert_w[0].reshape(D_MODEL, N_EXPERTS).T)
    br = jnp.zeros((ROUTER_ROWS,), F32)
    br = br.at[0:N_GROUPS].set(router_group_b[0])
    br = br.at[8:].set(router_expert_b[0].reshape(N_EXPERTS))
    br = jnp.broadcast_to(br[:, None], (ROUTER_ROWS, LANES))
    idx = jnp.arange(CHUNK)
    tri_f = (idx[None, :] <= idx[:, None]).astype(F32)
    tri_b = (idx[None, :] >= idx[:, None]).astype(F32)
    tidx = jnp.arange(TM_MIX)
    su = (tidx[:, None] < tidx[None, :]).astype(BF16)
    return dict(
        norm1_g=norm1_g[0].reshape(1, D_MODEL), w_main=w_main, w_gate=w_gate, gate_b=gb,
        pool_w=pool_w[0].astype(BF16), pool_scale=pool_scale[0].reshape(1, POOL_WIDTH),
        head_g=head_norm_g[0].reshape(1, MLSTM_WIDTH), w_out=w_out[0].astype(BF16),
        norm2_g=norm2_g[0].reshape(1, D_MODEL), wr=wr, br=br, tri_f=tri_f, tri_b=tri_b, su=su,
        e_gate=expert_w_gate[0], e_up=expert_w_up[0], e_down=expert_w_down[0],
        final_g=final_norm_g.reshape(1, D_MODEL),
    )


def _encoder(x, p):
    B, S, _ = x.shape
    T = B * S
    x2 = x.reshape(T, D_MODEL)
    u, q, k, v, o, gt = _inproj(x2, p["norm1_g"], p["w_main"], p["w_gate"], p["gate_b"])
    hf = _mlstm(q, k, v, gt, p["tri_f"], B, S, rev=False)
    m = _mlstm(q, k, v, gt, p["tri_b"], B, S, rev=True, hf=hf, o=o, head_g=p["head_g"])
    counts0 = jnp.zeros((N_EXPERTS, LANES), F32)
    x1, xn, eid, rank, ew, cnt = _mix(x2, u, m, p["pool_w"], p["pool_scale"], p["w_out"], p["norm2_g"],
                                      p["wr"], p["br"], p["su"], counts0, S)

    bm = BM_EXPERT
    nb = (2 * T) // bm + N_EXPERTS
    counts = cnt[:, 0].astype(I32)
    pcounts = ((counts + bm - 1) // bm) * bm
    pends = jnp.cumsum(pcounts)
    pstarts = pends - pcounts
    e_ids = jnp.arange(N_EXPERTS, dtype=I32)[:, None, None]
    dest = rank + jnp.sum(jnp.where(eid[None] == e_ids, pstarts[:, None, None], 0), axis=0)
    block_e = jnp.minimum(jnp.sum(pends[None, :] <= (jnp.arange(nb, dtype=I32) * bm)[:, None], axis=1),
                          N_EXPERTS - 1).astype(I32)
    n_valid = (pends[-1:] // bm).astype(I32)
    last_block = jnp.where(counts > 0, pends - bm, -1)
    tail_start = pends[-1] + jnp.arange(N_EXPERTS, dtype=I32) * bm
    tail_block = jnp.where(tail_start < nb * bm, tail_start, -1)
    zero_start = jnp.concatenate([last_block, tail_block]).astype(I32)
    nt = T // TM_ROWS
    dest0 = dest[0].reshape(nt, 1, TM_ROWS)
    dest1 = dest[1].reshape(nt, 1, TM_ROWS)

    rows = _push(zero_start, dest0, dest1, xn, nb * bm)
    yrows = _experts(block_e, n_valid, rows, p["e_gate"], p["e_up"], p["e_down"])
    y = _combine(dest0, dest1, x1, ew, p["final_g"], yrows)
    return y.reshape(B, S, D_MODEL)


def kernel(x_prompt, x_sample, norm1_g, w_in, pool_w, pool_scale, gate_b, head_norm_g, w_out, norm2_g,
           router_group_w, router_group_b, router_expert_w, router_expert_b,
           expert_w_gate, expert_w_up, expert_w_down, final_norm_g):
    p = _prepare_params(norm1_g, w_in, pool_w, pool_scale, gate_b, head_norm_g, w_out, norm2_g,
                        router_group_w, router_group_b, router_expert_w, router_expert_b,
                        expert_w_gate, expert_w_up, expert_w_down, final_norm_g)
    return (_encoder(x_prompt, p), _encoder(x_sample, p))
```

```python
import functools

import jax
import jax.numpy as jnp
from jax import lax
from jax.experimental import pallas as pl
from jax.experimental.pallas import tpu as pltpu

F32 = jnp.float32
BF16 = jnp.bfloat16
I32 = jnp.int32

D_MODEL = 1024
POOL_WIDTH = 512
POOL_WINDOWS = (2, 4, 8, 16)
POOL_GROUP = 128
POOL_HALO = 8
MLSTM_WIDTH = 512
HEADS = 4
HEAD_DIM = 128
CHUNK = 128
N_GATES = 16
N_GROUPS = 4
EPG = 8
N_EXPERTS = 32
D_EXPERT = 512
EPS = 1e-6
LANES = 128
ROUTER_ROWS = 8 + N_EXPERTS

MLSTM_CHUNKS_PER_STEP = 4
TM_PROJ = 512
TM_MIX = 512
TM_ROWS = 256
ROW_SUB = D_MODEL // LANES
BM_EXPERT = 256
VMEM_LIMIT = 56 * 1024 * 1024

HIGHEST = lax.Precision.HIGHEST


def _cparams(n_axes):
    return pltpu.CompilerParams(dimension_semantics=("arbitrary",) * n_axes,
                                vmem_limit_bytes=VMEM_LIMIT)


def _store_row_tiles(ref, val):
    rows = val.shape[0]
    for j in range(ROW_SUB):
        ref[pl.ds(j, rows, stride=ROW_SUB), :] = val[:, j * LANES:(j + 1) * LANES]


def _load_row_tiles(ref):
    rows = ref.shape[0] // ROW_SUB
    return jnp.concatenate([ref[pl.ds(j, rows, stride=ROW_SUB), :] for j in range(ROW_SUB)], axis=1)


def _row_tile(ref, r):
    return ref.at[pl.ds(pl.multiple_of(r * ROW_SUB, ROW_SUB), ROW_SUB), :]


def _inproj_kernel(x_ref, g_ref, w_ref, wg_ref, gb_ref, u_ref, q_ref, k_ref, v_ref, o_ref, gt_ref):
    x = x_ref[...]
    inv = lax.rsqrt(jnp.mean(x * x, axis=-1, keepdims=True) + EPS)
    xn = (x * inv * g_ref[...]).astype(BF16)

    def sec(i):
        return jnp.dot(xn, w_ref[:, i * 512:(i + 1) * 512], preferred_element_type=F32)

    u_ref[...] = sec(0)
    q_ref[...] = sec(1).astype(BF16)
    k_ref[...] = (sec(2) * (HEAD_DIM ** -0.5)).astype(BF16)
    v_ref[...] = sec(3).astype(BF16)
    o_ref[...] = sec(4)
    g = jnp.dot(xn, wg_ref[...], preferred_element_type=F32) + gb_ref[...]
    lane = lax.broadcasted_iota(I32, g.shape, 1)
    logsig = jnp.minimum(g, 0.0) - jnp.log1p(jnp.exp(-jnp.abs(g)))
    gt_ref[...] = jnp.where(lane >= 2 * HEADS, logsig, g)


def _inproj(x2, norm_g, w_main, w_gate, gate_b):
    T = x2.shape[0]
    tm = TM_PROJ
    tok = lambda i: (i, 0)
    fix = lambda i: (0, 0)
    outs = (
        jax.ShapeDtypeStruct((T, POOL_WIDTH), F32),
        jax.ShapeDtypeStruct((T, MLSTM_WIDTH), BF16),
        jax.ShapeDtypeStruct((T, MLSTM_WIDTH), BF16),
        jax.ShapeDtypeStruct((T, MLSTM_WIDTH), BF16),
        jax.ShapeDtypeStruct((T, MLSTM_WIDTH), F32),
        jax.ShapeDtypeStruct((T, LANES), F32),
    )
    return pl.pallas_call(
        _inproj_kernel,
        grid=(T // tm,),
        in_specs=[
            pl.BlockSpec((tm, D_MODEL), tok),
            pl.BlockSpec((1, D_MODEL), fix),
            pl.BlockSpec((D_MODEL, 5 * 512), fix),
            pl.BlockSpec((D_MODEL, LANES), fix),
            pl.BlockSpec((1, LANES), fix),
        ],
        out_specs=[
            pl.BlockSpec((tm, 512), tok), pl.BlockSpec((tm, 512), tok), pl.BlockSpec((tm, 512), tok),
            pl.BlockSpec((tm, 512), tok), pl.BlockSpec((tm, 512), tok), pl.BlockSpec((tm, LANES), tok),
        ],
        out_shape=outs,
        compiler_params=_cparams(1),
        name="inproj",
    )(x2, norm_g, w_main, w_gate, gate_b)


def _mlstm_kernel(*refs, rev):
    if rev:
        (q_ref, k_ref, v_ref, gt_ref, tri_ref, hf_ref, o_ref, hg_ref, out_ref, c_sc, m_sc, a_sc, upd_sc) = refs
    else:
        (q_ref, k_ref, v_ref, gt_ref, tri_ref, out_ref, c_sc, m_sc, a_sc, upd_sc) = refs
    L = CHUNK

    @pl.when(pl.program_id(1) == 0)
    def _():
        c_sc[...] = jnp.zeros_like(c_sc)
        m_sc[...] = jnp.zeros_like(m_sc)

    tri = tri_ref[...]
    mask = tri > 0.0
    lane = lax.broadcasted_iota(I32, (L, LANES), 1)
    ones_col = [(lane == HEADS * rev + h).astype(BF16) for h in range(HEADS)]

    nch = MLSTM_CHUNKS_PER_STEP
    order = [nch - 1 - i if rev else i for i in range(nch)]
    hd = range(HEADS)
    cs = [HEADS * rev + h for h in hd]
    hsl = [slice(h * HEAD_DIM, (h + 1) * HEAD_DIM) for h in hd]
    rows = [slice(ci * L, (ci + 1) * L) for ci in range(nch)]
    pairs = [(ci, h) for ci in order for h in hd]
    col = lambda t, c: t[:, c:c + 1]

    g = {ci: gt_ref[rows[ci], :] for ci in order}
    b_all = {ci: jnp.dot(tri, g[ci], precision=HIGHEST, preferred_element_type=F32) for ci in order}
    ba = {ci: pltpu.roll(b_all[ci], LANES - 2 * HEADS, axis=1) for ci in order}
    r_rows = {ci: (g[ci] - ba[ci]).T for ci in order}
    b_last = {ci: (ba[ci][0:1, :] if rev else ba[ci][L - 1:L, :]) for ci in order}
    logu = {ci: b_last[ci] - ba[ci] + g[ci] for ci in order}
    mu = {ci: jnp.max(logu[ci], axis=0, keepdims=True) for ci in order}
    u_loc = {ci: jnp.exp(logu[ci] - mu[ci]) for ci in order}

    v_aug = {p: jnp.concatenate([v_ref[rows[p[0]], hsl[p[1]]], ones_col[p[1]]], axis=1) for p in pairs}
    qk = {p: lax.dot_general(q_ref[rows[p[0]], hsl[p[1]]], k_ref[rows[p[0]], hsl[p[1]]],
                             (((1,), (1,)), ((), ())), preferred_element_type=F32) for p in pairs}
    ku = {p: (k_ref[rows[p[0]], hsl[p[1]]].astype(F32) * col(u_loc[p[0]], cs[p[1]])).astype(BF16) for p in pairs}
    for p in pairs:
        upd_sc[p[0] * HEADS + p[1]] = lax.dot_general(ku[p], v_aug[p], (((0,), (0,)), ((), ())),
                                                      preferred_element_type=F32)
    logw = {p: jnp.where(mask, col(ba[p[0]], cs[p[1]]) + r_rows[p[0]][cs[p[1]]:cs[p[1]] + 1, :], -jnp.inf)
            for p in pairs}
    mw = {p: jnp.max(logw[p], axis=-1, keepdims=True) for p in pairs}
    s = {p: (qk[p] * jnp.exp(logw[p] - mw[p])).astype(BF16) for p in pairs}
    for p in pairs:
        a_sc[p[0] * HEADS + p[1]] = jnp.dot(s[p], v_aug[p], preferred_element_type=F32)
    mw_t = {}
    for ci in order:
        t = jnp.zeros((L, LANES), F32)
        for h in hd:
            t = jnp.where(lane == cs[h], mw[(ci, h)], t)
        mw_t[ci] = t

    for ci in order:
        m_prev = m_sc[...]
        m_new = jnp.maximum(b_last[ci] + m_prev, mu[ci])
        decay = jnp.exp(b_last[ci] + m_prev - m_new)
        u_scale = jnp.exp(mu[ci] - m_new)
        m_sc[...] = m_new
        inter_log = ba[ci] + m_prev
        m_row = jnp.maximum(inter_log, mw_t[ci])
        w_intra = jnp.exp(mw_t[ci] - m_row)
        w_cross = jnp.exp(inter_log - m_row)
        c_old = [c_sc[h] for h in hd]
        x = [jnp.dot(q_ref[rows[ci], hsl[h]], c_old[h].astype(BF16), preferred_element_type=F32) for h in hd]
        for h in hd:
            c_sc[h] = col(decay, cs[h]) * c_old[h] + col(u_scale, cs[h]) * upd_sc[ci * HEADS + h]
        den_loc = ((a_sc[ci * HEADS + 0, :, HEAD_DIM:] + a_sc[ci * HEADS + 1, :, HEAD_DIM:])
                   + (a_sc[ci * HEADS + 2, :, HEAD_DIM:] + a_sc[ci * HEADS + 3, :, HEAD_DIM:]))
        den_cross = (x[0][:, HEAD_DIM:] + x[1][:, HEAD_DIM:]) + (x[2][:, HEAD_DIM:] + x[3][:, HEAD_DIM:])
        den = w_intra * den_loc + w_cross * den_cross
        rden = 1.0 / jnp.maximum(jnp.abs(den), jnp.exp(-m_row))
        f_intra = w_intra * rden
        f_cross = w_cross * rden
        hh = [col(f_intra, cs[h]) * a_sc[ci * HEADS + h, :, :HEAD_DIM] + col(f_cross, cs[h]) * x[h][:, :HEAD_DIM]
              for h in hd]
        if rev:
            hsum = [hf_ref[rows[ci], hsl[h]] + hh[h] for h in hd]
            inv = [lax.rsqrt(jnp.mean(hsum[h] * hsum[h], axis=-1, keepdims=True) + EPS) for h in hd]
            for h in hd:
                out_ref[rows[ci], hsl[h]] = (jax.nn.sigmoid(o_ref[rows[ci], hsl[h]])
                                             * (hsum[h] * inv[h] * hg_ref[:, hsl[h]])).astype(out_ref.dtype)
        else:
            for h in hd:
                out_ref[rows[ci], hsl[h]] = hh[h]


def _mlstm(q, k, v, gt, tri, B, S, rev, hf=None, o=None, head_g=None):
    T = B * S
    rows = CHUNK * MLSTM_CHUNKS_PER_STEP
    nc = S // rows
    if rev:
        tok = lambda b, c: (b * nc + (nc - 1 - c), 0)
    else:
        tok = lambda b, c: (b * nc + c, 0)
    fix = lambda b, c: (0, 0)
    blk = lambda w: pl.BlockSpec((rows, w), tok)
    in_specs = [blk(512), blk(512), blk(512), blk(LANES), pl.BlockSpec((CHUNK, CHUNK), fix)]
    args = [q, k, v, gt, tri]
    if rev:
        in_specs += [blk(512), blk(512), pl.BlockSpec((1, 512), fix)]
        args += [hf, o, head_g]
    return pl.pallas_call(
        functools.partial(_mlstm_kernel, rev=rev),
        grid=(B, nc),
        in_specs=in_specs,
        out_specs=blk(512),
        out_shape=jax.ShapeDtypeStruct((T, MLSTM_WIDTH), BF16 if rev else F32),
        scratch_shapes=[pltpu.VMEM((HEADS, HEAD_DIM, 2 * HEAD_DIM), F32), pltpu.VMEM((1, LANES), F32),
                        pltpu.VMEM((MLSTM_CHUNKS_PER_STEP * HEADS, CHUNK, 2 * HEAD_DIM), F32),
                        pltpu.VMEM((MLSTM_CHUNKS_PER_STEP * HEADS, HEAD_DIM, 2 * HEAD_DIM), F32)],
        compiler_params=_cparams(2),
        name="mlstm_bwd" if rev else "mlstm_fwd",
    )(*args)


def _mix_kernel(x_ref, u_ref, up_ref, un_ref, m_ref, pw_ref, ps_ref, wo_ref, g2_ref, wr_ref, br_ref, su_ref,
                cin_ref, x1_ref, xn_ref, eid_ref, rank_ref, ew_ref, cnt_ref, ext_sc, run_sc, *, seq_len):
    tm = TM_MIX
    i = pl.program_id(0)
    tps = seq_len // tm
    it = i % tps

    @pl.when(i == 0)
    def _():
        run_sc[...] = cin_ref[...]

    ext_sc[0:POOL_HALO, :] = jnp.where(it == 0, 0.0, up_ref[...])
    ext_sc[POOL_HALO:POOL_HALO + tm, :] = u_ref[...]
    ext_sc[POOL_HALO + tm:, :] = jnp.where(it == tps - 1, 0.0, un_ref[...])
    t_seq = it * tm + lax.broadcasted_iota(I32, (tm, 1), 0)
    a_parts = []
    for gi, w in enumerate(POOL_WINDOWS):
        ls = slice(gi * POOL_GROUP, (gi + 1) * POOL_GROUP)
        acc = None
        for kk in range(-(w // 2), w // 2):
            piece = ext_sc[POOL_HALO + kk:POOL_HALO + kk + tm, ls]
            acc = piece if acc is None else acc + piece
        cnt = (jnp.minimum(t_seq + w // 2, seq_len) - jnp.maximum(t_seq - w // 2, 0)).astype(F32)
        d = acc / cnt - u_ref[:, ls]
        a_parts.append(jnp.dot(d.astype(BF16), pw_ref[gi], preferred_element_type=F32))
    a = (jnp.concatenate(a_parts, axis=1) * ps_ref[...]).astype(BF16)

    x1 = (x_ref[...]
          + jnp.dot(a, wo_ref[0:POOL_WIDTH, :], preferred_element_type=F32)
          + jnp.dot(m_ref[...], wo_ref[POOL_WIDTH:, :], preferred_element_type=F32))
    x1_ref[...] = x1
    xn = x1 * lax.rsqrt(jnp.mean(x1 * x1, axis=-1, keepdims=True) + EPS) * g2_ref[...]
    _store_row_tiles(xn_ref, xn)

    lg = lax.dot_general(wr_ref[...], xn.astype(BF16), (((1,), (1,)), ((), ())),
                         preferred_element_type=F32) + br_ref[:, 0:1]
    gl = lg[0:N_GROUPS, :]
    gmax = jnp.max(gl, axis=0, keepdims=True)
    giota = lax.broadcasted_iota(I32, gl.shape, 0).astype(F32)
    g_idx = jnp.min(jnp.where(gl == gmax, giota, float(N_GROUPS)), axis=0, keepdims=True)
    g_w = 1.0 / jnp.sum(jnp.exp(gl - gmax), axis=0, keepdims=True)
    el = lg[8:8 + EPG, :]
    for gg in range(1, N_GROUPS):
        el = jnp.where(g_idx == float(gg), lg[8 + gg * EPG:8 + (gg + 1) * EPG, :], el)
    eiota = lax.broadcasted_iota(I32, el.shape, 0).astype(F32)
    top1 = jnp.max(el, axis=0, keepdims=True)
    i1 = jnp.min(jnp.where(el == top1, eiota, float(EPG)), axis=0, keepdims=True)
    el2 = jnp.where(eiota == i1, -jnp.inf, el)
    top2 = jnp.max(el2, axis=0, keepdims=True)
    i2 = jnp.min(jnp.where(el2 == top2, eiota, float(EPG)), axis=0, keepdims=True)
    ex = jnp.exp(top2 - top1)
    w1 = g_w / (1.0 + ex)
    w2 = g_w * ex / (1.0 + ex)
    e1 = g_idx * float(EPG) + i1
    e2 = g_idx * float(EPG) + i2
    eid_ref[0:1, :] = e1.astype(I32)
    eid_ref[1:2, :] = e2.astype(I32)

    xiota = lax.broadcasted_iota(I32, (N_EXPERTS, tm), 0).astype(F32)
    oh1 = xiota == e1
    oh2 = xiota == e2
    oh = jnp.where(oh1 | oh2, 1.0, 0.0)
    before = jnp.dot(oh.astype(BF16), su_ref[...], preferred_element_type=F32)
    base = run_sc[:, 0:1] + before
    rank_ref[0:1, :] = jnp.sum(jnp.where(oh1, base, 0.0), axis=0, keepdims=True).astype(I32)
    rank_ref[1:2, :] = jnp.sum(jnp.where(oh2, base, 0.0), axis=0, keepdims=True).astype(I32)
    run_new = run_sc[...] + jnp.sum(oh, axis=1, keepdims=True)
    run_sc[...] = run_new
    cnt_ref[...] = run_new

    riota = lax.broadcasted_iota(I32, (LANES, tm), 0)
    wt = jnp.where(riota == 0, w1, jnp.where(riota == 1, w2, 0.0))
    ew_ref[...] = wt.T


def _mix(x2, u, m, pool_w, pool_scale, w_out, norm2_g, wr, br, su, counts_in, seq_len):
    T = x2.shape[0]
    tm = TM_MIX
    hb = tm // POOL_HALO
    nhb = T // POOL_HALO
    tok = lambda i: (i, 0)
    fix = lambda i: (0, 0)
    outs = (
        jax.ShapeDtypeStruct((T, D_MODEL), F32),
        jax.ShapeDtypeStruct((T * ROW_SUB, LANES), F32),
        jax.ShapeDtypeStruct((2, T), I32),
        jax.ShapeDtypeStruct((2, T), I32),
        jax.ShapeDtypeStruct((T, LANES), F32),
        jax.ShapeDtypeStruct((N_EXPERTS, LANES), F32),
    )
    return pl.pallas_call(
        functools.partial(_mix_kernel, seq_len=seq_len),
        grid=(T // tm,),
        in_specs=[
            pl.BlockSpec((tm, D_MODEL), tok),
            pl.BlockSpec((tm, POOL_WIDTH), tok),
            pl.BlockSpec((POOL_HALO, POOL_WIDTH), lambda i: (jnp.maximum(i * hb - 1, 0), 0)),
            pl.BlockSpec((POOL_HALO, POOL_WIDTH), lambda i: (jnp.minimum((i + 1) * hb, nhb - 1), 0)),
            pl.BlockSpec((tm, MLSTM_WIDTH), tok),
            pl.BlockSpec((len(POOL_WINDOWS), POOL_GROUP, POOL_GROUP), lambda i: (0, 0, 0)),
            pl.BlockSpec((1, POOL_WIDTH), fix),
            pl.BlockSpec((D_MODEL, D_MODEL), fix),
            pl.BlockSpec((1, D_MODEL), fix),
            pl.BlockSpec((ROUTER_ROWS, D_MODEL), fix),
            pl.BlockSpec((ROUTER_ROWS, LANES), fix),
            pl.BlockSpec((tm, tm), fix),
            pl.BlockSpec((N_EXPERTS, LANES), fix),
        ],
        out_specs=[
            pl.BlockSpec((tm, D_MODEL), tok),
            pl.BlockSpec((tm * ROW_SUB, LANES), tok),
            pl.BlockSpec((2, tm), lambda i: (0, i)),
            pl.BlockSpec((2, tm), lambda i: (0, i)),
            pl.BlockSpec((tm, LANES), tok),
            pl.BlockSpec((N_EXPERTS, LANES), fix),
        ],
        out_shape=outs,
        scratch_shapes=[pltpu.VMEM((tm + 2 * POOL_HALO, POOL_WIDTH), F32), pltpu.VMEM((N_EXPERTS, LANES), F32)],
        compiler_params=_cparams(1),
        name="mix_router",
    )(x2, u, u, u, m, pool_w, pool_scale, w_out, norm2_g, wr, br, su, counts_in)


def _row_copy(src, dst, sem):
    return pltpu.make_async_copy(src, dst, sem)


def _push_kernel(zs_ref, d0_ref, d1_ref, xn_ref, rows_ref, zero_sc, sem, zsem):
    tm = TM_ROWS

    @pl.when(pl.program_id(0) == 0)
    def _():
        zero_sc[...] = jnp.zeros_like(zero_sc)

        def fill(e):
            start = pl.multiple_of(jnp.maximum(zs_ref[e], 0) * ROW_SUB, ROW_SUB)
            return _row_copy(zero_sc, rows_ref.at[pl.ds(start, BM_EXPERT * ROW_SUB), :], zsem)

        for e in range(2 * N_EXPERTS):
            @pl.when(zs_ref[e] >= 0)
            def _():
                fill(e).start()
        for e in range(2 * N_EXPERTS):
            @pl.when(zs_ref[e] >= 0)
            def _():
                fill(e).wait()

    def issue(t, carry):
        src = _row_tile(xn_ref, t)
        _row_copy(src, _row_tile(rows_ref, d0_ref[0, 0, t]), sem).start(priority=0)
        _row_copy(src, _row_tile(rows_ref, d1_ref[0, 0, t]), sem).start(priority=1)
        return carry

    lax.fori_loop(0, tm, issue, 0, unroll=4)

    def drain(t, carry):
        src = _row_tile(xn_ref, t)
        _row_copy(src, _row_tile(rows_ref, d0_ref[0, 0, t]), sem).wait()
        _row_copy(src, _row_tile(rows_ref, d1_ref[0, 0, t]), sem).wait()
        return carry

    lax.fori_loop(0, tm, drain, 0, unroll=4)


def _push(zero_start, dest0, dest1, xn, n_rows):
    T = xn.shape[0] // ROW_SUB
    tm = TM_ROWS
    smem_blk = pl.BlockSpec((1, 1, tm), lambda i, zs: (i, 0, 0), memory_space=pltpu.SMEM)
    grid_spec = pltpu.PrefetchScalarGridSpec(
        num_scalar_prefetch=1,
        grid=(T // tm,),
        in_specs=[smem_blk, smem_blk, pl.BlockSpec((tm * ROW_SUB, LANES), lambda i, zs: (i, 0))],
        out_specs=pl.BlockSpec(memory_space=pl.ANY),
        scratch_shapes=[pltpu.VMEM((BM_EXPERT * ROW_SUB, LANES), F32), pltpu.SemaphoreType.DMA(()),
                        pltpu.SemaphoreType.DMA(())],
    )
    return pl.pallas_call(
        _push_kernel,
        grid_spec=grid_spec,
        out_shape=jax.ShapeDtypeStruct((n_rows * ROW_SUB, LANES), F32),
        compiler_params=_cparams(1),
        name="row_push",
    )(zero_start, dest0, dest1, xn)


def _expert_kernel(be_ref, nv_ref, x_ref, wg_ref, wu_ref, wd_ref, y_ref, wg_sc, wu_sc, wd_sc):
    i = pl.program_id(0)
    valid = i < nv_ref[0]
    new_expert = (i == 0) | (be_ref[i] != be_ref[jnp.maximum(i - 1, 0)])

    @pl.when(valid & new_expert)
    def _():
        wg_sc[...] = wg_ref[0].astype(BF16)
        wu_sc[...] = wu_ref[0].astype(BF16)
        wd_sc[...] = wd_ref[0].astype(BF16)

    @pl.when(valid)
    def _():
        xb = _load_row_tiles(x_ref).astype(BF16)
        gate = jnp.dot(xb, wg_sc[...], preferred_element_type=F32)
        up = jnp.dot(xb, wu_sc[...], preferred_element_type=F32)
        hb = (gate * jax.nn.sigmoid(gate) * up).astype(BF16)
        _store_row_tiles(y_ref, jnp.dot(hb, wd_sc[...], preferred_element_type=F32))

    @pl.when(jnp.logical_not(valid))
    def _():
        y_ref[...] = jnp.zeros_like(y_ref)


def _experts(block_e, n_valid, rows, w_gate, w_up, w_down):
    P = rows.shape[0] // ROW_SUB
    bm = BM_EXPERT
    last = lambda i, nv: jnp.minimum(i, jnp.maximum(nv[0] - 1, 0))
    row_blk = lambda i, be, nv: (last(i, nv), 0)
    w_blk = lambda i, be, nv: (be[last(i, nv)], 0, 0)
    grid_spec = pltpu.PrefetchScalarGridSpec(
        num_scalar_prefetch=2,
        grid=(P // bm,),
        in_specs=[
            pl.BlockSpec((bm * ROW_SUB, LANES), row_blk),
            pl.BlockSpec((1, D_MODEL, D_EXPERT), w_blk),
            pl.BlockSpec((1, D_MODEL, D_EXPERT), w_blk),
            pl.BlockSpec((1, D_EXPERT, D_MODEL), w_blk),
        ],
        out_specs=pl.BlockSpec((bm * ROW_SUB, LANES), lambda i, be, nv: (i, 0)),
        scratch_shapes=[pltpu.VMEM((D_MODEL, D_EXPERT), BF16), pltpu.VMEM((D_MODEL, D_EXPERT), BF16),
                        pltpu.VMEM((D_EXPERT, D_MODEL), BF16)],
    )
    return pl.pallas_call(
        _expert_kernel,
        grid_spec=grid_spec,
        out_shape=jax.ShapeDtypeStruct((P * ROW_SUB, LANES), F32),
        compiler_params=_cparams(1),
        name="experts",
    )(block_e, n_valid, rows, w_gate, w_up, w_down)


def _combine_kernel(d0_ref, d1_ref, d0n_ref, d1n_ref, x1_ref, ew_ref, gf_ref, y_ref, out_ref, buf, sem):
    tm = TM_ROWS
    i = pl.program_id(0)
    slot = i % 2

    def copies(d0, d1, sl, t):
        return (_row_copy(_row_tile(y_ref, d0[0, 0, t]), _row_tile(buf.at[sl, 0], t), sem.at[sl]),
                _row_copy(_row_tile(y_ref, d1[0, 0, t]), _row_tile(buf.at[sl, 1], t), sem.at[sl]))

    def gather_start(d0, d1, sl):
        def body(t, carry):
            c0, c1 = copies(d0, d1, sl, t)
            c0.start(priority=0)
            c1.start(priority=1)
            return carry
        lax.fori_loop(0, tm, body, 0, unroll=4)

    @pl.when(i == 0)
    def _():
        gather_start(d0_ref, d1_ref, 0)

    @pl.when(i + 1 < pl.num_programs(0))
    def _():
        gather_start(d0n_ref, d1n_ref, 1 - slot)

    def drain(t, carry):
        c0, c1 = copies(d0_ref, d1_ref, slot, t)
        c0.wait()
        c1.wait()
        return carry

    lax.fori_loop(0, tm, drain, 0, unroll=4)

    ew = ew_ref[...]
    x = (x1_ref[...] + _load_row_tiles(buf.at[slot, 0]) * ew[:, 0:1]
         + _load_row_tiles(buf.at[slot, 1]) * ew[:, 1:2])
    out_ref[...] = x * lax.rsqrt(jnp.mean(x * x, axis=-1, keepdims=True) + EPS) * gf_ref[...]


def _combine(dest0, dest1, x1, ew, final_g, yrows):
    T = x1.shape[0]
    tm = TM_ROWS
    nt = T // tm
    smem_blk = pl.BlockSpec((1, 1, tm), lambda i: (i, 0, 0), memory_space=pltpu.SMEM)
    smem_next = pl.BlockSpec((1, 1, tm), lambda i: (jnp.minimum(i + 1, nt - 1), 0, 0), memory_space=pltpu.SMEM)
    tok = lambda i: (i, 0)
    return pl.pallas_call(
        _combine_kernel,
        grid=(nt,),
        in_specs=[smem_blk, smem_blk, smem_next, smem_next,
                  pl.BlockSpec((tm, D_MODEL), tok),
                  pl.BlockSpec((tm, LANES), tok),
                  pl.BlockSpec((1, D_MODEL), lambda i: (0, 0)),
                  pl.BlockSpec(memory_space=pl.ANY)],
        out_specs=pl.BlockSpec((tm, D_MODEL), tok),
        out_shape=jax.ShapeDtypeStruct((T, D_MODEL), F32),
        scratch_shapes=[pltpu.VMEM((2, 2, tm * ROW_SUB, LANES), F32), pltpu.SemaphoreType.DMA((2,))],
        compiler_params=_cparams(1),
        name="combine",
    )(dest0, dest1, dest0, dest1, x1, ew, final_g, yrows)


def _prepare_params(norm1_g, w_in, pool_w, pool_scale, gate_b, head_norm_g, w_out, norm2_g,
                    router_group_w, router_group_b, router_expert_w, router_expert_b,
                    expert_w_gate, expert_w_up, expert_w_down, final_norm_g):
    w = w_in[0]
    n_main = 5 * 512
    w_main = w[:, :n_main].astype(BF16)
    w_gate = jnp.pad(w[:, n_main:], ((0, 0), (0, LANES - N_GATES))).astype(BF16)
    gb = jnp.pad(gate_b[0], (0, LANES - N_GATES)).reshape(1, LANES)
    wr = jnp.zeros((ROUTER_ROWS, D_MODEL), F32)
    wr = wr.at[0:N_GROUPS].set(router_group_w[0].T)
    wr = wr.at[8:].set(router_expert_w[0].reshape(D_MODEL, N_EXPERTS).T)
    br = jnp.zeros((ROUTER_ROWS,), F32)
    br = br.at[0:N_GROUPS].set(router_group_b[0])
    br = br.at[8:].set(router_expert_b[0].reshape(N_EXPERTS))
    br = jnp.broadcast_to(br[:, None], (ROUTER_ROWS, LANES))
    idx = jnp.arange(CHUNK)
    tri_f = (idx[None, :] <= idx[:, None]).astype(F32)
    tri_b = (idx[None, :] >= idx[:, None]).astype(F32)
    tidx = jnp.arange(TM_MIX)
    su = (tidx[:, None] < tidx[None, :]).astype(BF16)
    return dict(
        norm1_g=norm1_g[0].reshape(1, D_MODEL), w_main=w_main, w_gate=w_gate, gate_b=gb,
        pool_w=pool_w[0].astype(BF16), pool_scale=pool_scale[0].reshape(1, POOL_WIDTH),
        head_g=head_norm_g[0].reshape(1, MLSTM_WIDTH), w_out=w_out[0].astype(BF16),
        norm2_g=norm2_g[0].reshape(1, D_MODEL), wr=wr.astype(BF16), br=br, tri_f=tri_f, tri_b=tri_b, su=su,
        e_gate=expert_w_gate[0], e_up=expert_w_up[0], e_down=expert_w_down[0],
        final_g=final_norm_g.reshape(1, D_MODEL),
    )


def _encoder(x, p):
    B, S, _ = x.shape
    T = B * S
    x2 = x.reshape(T, D_MODEL)
    u, q, k, v, o, gt = _inproj(x2, p["norm1_g"], p["w_main"], p["w_gate"], p["gate_b"])
    hf = _mlstm(q, k, v, gt, p["tri_f"], B, S, rev=False)
    m = _mlstm(q, k, v, gt, p["tri_b"], B, S, rev=True, hf=hf, o=o, head_g=p["head_g"])
    counts0 = jnp.zeros((N_EXPERTS, LANES), F32)
    x1, xn, eid, rank, ew, cnt = _mix(x2, u, m, p["pool_w"], p["pool_scale"], p["w_out"], p["norm2_g"],
                                      p["wr"], p["br"], p["su"], counts0, S)

    bm = BM_EXPERT
    nb = (2 * T) // bm + N_EXPERTS
    counts = cnt[:, 0].astype(I32)
    pcounts = ((counts + bm - 1) // bm) * bm
    pends = jnp.cumsum(pcounts)
    pstarts = pends - pcounts
    e_ids = jnp.arange(N_EXPERTS, dtype=I32)[:, None, None]
    dest = rank + jnp.sum(jnp.where(eid[None] == e_ids, pstarts[:, None, None], 0), axis=0)
    block_e = jnp.minimum(jnp.sum(pends[None, :] <= (jnp.arange(nb, dtype=I32) * bm)[:, None], axis=1),
                          N_EXPERTS - 1).astype(I32)
    n_valid = (pends[-1:] // bm).astype(I32)
    last_block = jnp.where(counts > 0, pends - bm, -1)
    tail_start = pends[-1] + jnp.arange(N_EXPERTS, dtype=I32) * bm
    tail_block = jnp.where(tail_start < nb * bm, tail_start, -1)
    zero_start = jnp.concatenate([last_block, tail_block]).astype(I32)
    nt = T // TM_ROWS
    dest0 = dest[0].reshape(nt, 1, TM_ROWS)
    dest1 = dest[1].reshape(nt, 1, TM_ROWS)

    rows = _push(zero_start, dest0, dest1, xn, nb * bm)
    yrows = _experts(block_e, n_valid, rows, p["e_gate"], p["e_up"], p["e_down"])
    y = _combine(dest0, dest1, x1, ew, p["final_g"], yrows)
    return y.reshape(B, S, D_MODEL)


def kernel(x_prompt, x_sample, norm1_g, w_in, pool_w, pool_scale, gate_b, head_norm_g, w_out, norm2_g,
           router_group_w, router_group_b, router_expert_w, router_expert_b,
           expert_w_gate, expert_w_up, expert_w_down, final_norm_g):
    p = _prepare_params(norm1_g, w_in, pool_w, pool_scale, gate_b, head_norm_g, w_out, norm2_g,
                        router_group_w, router_group_b, router_expert_w, router_expert_b,
                        expert_w_gate, expert_w_up, expert_w_down, final_norm_g)
    return (_encoder(x_prompt, p), _encoder(x_sample, p))
```

```python
import functools

import jax
import jax.numpy as jnp
from jax import lax
from jax.experimental import pallas as pl
from jax.experimental.pallas import tpu as pltpu

F32 = jnp.float32
BF16 = jnp.bfloat16
I32 = jnp.int32

D_MODEL = 1024
POOL_WIDTH = 512
POOL_WINDOWS = (2, 4, 8, 16)
POOL_GROUP = 128
POOL_HALO = 8
MLSTM_WIDTH = 512
HEADS = 4
HEAD_DIM = 128
CHUNK = 128
N_GATES = 16
N_GROUPS = 4
EPG = 8
N_EXPERTS = 32
D_EXPERT = 512
EPS = 1e-6
LANES = 128
ROUTER_ROWS = 8 + N_EXPERTS

MLSTM_CHUNKS_PER_STEP = 4
STATE_ROWS = HEAD_DIM + 16
TM_PROJ = 512
TM_MIX = 512
TM_ROWS = 256
ROW_SUB = D_MODEL // LANES
BM_EXPERT = 512
VMEM_LIMIT = 56 * 1024 * 1024

HIGHEST = lax.Precision.HIGHEST


def _cparams(n_axes):
    return pltpu.CompilerParams(dimension_semantics=("arbitrary",) * n_axes,
                                vmem_limit_bytes=VMEM_LIMIT)


def _store_row_tiles(ref, val):
    rows = val.shape[0]
    for j in range(ROW_SUB):
        ref[pl.ds(j, rows, stride=ROW_SUB), :] = val[:, j * LANES:(j + 1) * LANES]


def _load_row_tiles(ref):
    rows = ref.shape[0] // ROW_SUB
    return jnp.concatenate([ref[pl.ds(j, rows, stride=ROW_SUB), :] for j in range(ROW_SUB)], axis=1)


def _row_tile(ref, r):
    return ref.at[pl.ds(pl.multiple_of(r * ROW_SUB, ROW_SUB), ROW_SUB), :]


def _inproj_kernel(x_ref, g_ref, w_ref, wt_ref, wg_ref, gb_ref, u_ref, q_ref, k_ref, vt_ref, ot_ref, gt_ref):
    x = x_ref[...]
    inv = lax.rsqrt(jnp.mean(x * x, axis=-1, keepdims=True) + EPS)
    xn = (x * inv * g_ref[...]).astype(BF16)

    def sec(i):
        return jnp.dot(xn, w_ref[:, i * 512:(i + 1) * 512], preferred_element_type=F32)

    def sec_t(i):
        return lax.dot_general(wt_ref[i * 512:(i + 1) * 512, :], xn, (((1,), (1,)), ((), ())),
                               preferred_element_type=F32)

    u_ref[...] = sec(0)
    q_ref[...] = sec(1).astype(BF16)
    k_ref[...] = (sec(2) * (HEAD_DIM ** -0.5)).astype(BF16)
    vt_ref[...] = sec_t(0).astype(BF16)
    ot_ref[...] = sec_t(1)
    g = jnp.dot(xn, wg_ref[...], preferred_element_type=F32) + gb_ref[...]
    lane = lax.broadcasted_iota(I32, g.shape, 1)
    logsig = jnp.minimum(g, 0.0) - jnp.log1p(jnp.exp(-jnp.abs(g)))
    gt_ref[...] = jnp.where(lane >= 2 * HEADS, logsig, g)


def _inproj(x2, norm_g, w_main, w_vo_t, w_gate, gate_b):
    T = x2.shape[0]
    tm = TM_PROJ
    tok = lambda i: (i, 0)
    tok_t = lambda i: (0, i)
    fix = lambda i: (0, 0)
    outs = (
        jax.ShapeDtypeStruct((T, POOL_WIDTH), F32),
        jax.ShapeDtypeStruct((T, MLSTM_WIDTH), BF16),
        jax.ShapeDtypeStruct((T, MLSTM_WIDTH), BF16),
        jax.ShapeDtypeStruct((MLSTM_WIDTH, T), BF16),
        jax.ShapeDtypeStruct((MLSTM_WIDTH, T), F32),
        jax.ShapeDtypeStruct((T, LANES), F32),
    )
    return pl.pallas_call(
        _inproj_kernel,
        grid=(T // tm,),
        in_specs=[
            pl.BlockSpec((tm, D_MODEL), tok),
            pl.BlockSpec((1, D_MODEL), fix),
            pl.BlockSpec((D_MODEL, 3 * 512), fix),
            pl.BlockSpec((2 * 512, D_MODEL), fix),
            pl.BlockSpec((D_MODEL, LANES), fix),
            pl.BlockSpec((1, LANES), fix),
        ],
        out_specs=[
            pl.BlockSpec((tm, 512), tok), pl.BlockSpec((tm, 512), tok), pl.BlockSpec((tm, 512), tok),
            pl.BlockSpec((512, tm), tok_t), pl.BlockSpec((512, tm), tok_t), pl.BlockSpec((tm, LANES), tok),
        ],
        out_shape=outs,
        compiler_params=_cparams(1),
        name="inproj",
    )(x2, norm_g, w_main, w_vo_t, w_gate, gate_b)


def _mlstm_kernel(*refs, rev):
    if rev:
        (q_ref, k_ref, vt_ref, gt_ref, tri_ref, trit_ref, hft_ref, ot_ref, hgt_ref, out_ref,
         c_sc, m_sc, a_sc, upd_sc) = refs
    else:
        (q_ref, k_ref, vt_ref, gt_ref, tri_ref, trit_ref, out_ref, c_sc, m_sc, a_sc, upd_sc) = refs
    L = CHUNK

    @pl.when(pl.program_id(1) == 0)
    def _():
        c_sc[...] = jnp.zeros_like(c_sc)
        m_sc[...] = jnp.zeros_like(m_sc)

    tri = tri_ref[...]
    mask_t = trit_ref[...] > 0.0
    sub8 = lax.broadcasted_iota(I32, (8, L), 0)
    sub_aug = lax.broadcasted_iota(I32, (STATE_ROWS - HEAD_DIM, L), 0)

    nch = MLSTM_CHUNKS_PER_STEP
    order = [nch - 1 - i if rev else i for i in range(nch)]
    hd = range(HEADS)
    cs = [HEADS * rev + h for h in hd]
    hsl = [slice(h * HEAD_DIM, (h + 1) * HEAD_DIM) for h in hd]
    rows = [slice(ci * L, (ci + 1) * L) for ci in range(nch)]
    pairs = [(ci, h) for ci in order for h in hd]
    row = lambda t, c: t[c:c + 1, :]
    last = 0 if rev else L - 1

    g = {ci: gt_ref[rows[ci], :] for ci in order}
    b_all = {ci: jnp.dot(tri, g[ci], precision=HIGHEST, preferred_element_type=F32) for ci in order}
    ba = {ci: pltpu.roll(b_all[ci], LANES - 2 * HEADS, axis=1) for ci in order}
    r_col = {ci: g[ci] - ba[ci] for ci in order}
    i_r = {ci: g[ci].T[0:8, :] for ci in order}
    b_r = {ci: ba[ci].T[0:8, :] for ci in order}
    b_last = {ci: b_r[ci][:, last:last + 1] for ci in order}
    logu = {ci: b_last[ci] - b_r[ci] + i_r[ci] for ci in order}
    mu = {ci: jnp.max(logu[ci], axis=1, keepdims=True) for ci in order}
    u_r = {ci: jnp.exp(logu[ci] - mu[ci]) for ci in order}

    kq = {p: lax.dot_general(k_ref[rows[p[0]], hsl[p[1]]], q_ref[rows[p[0]], hsl[p[1]]],
                             (((1,), (1,)), ((), ())), preferred_element_type=F32) for p in pairs}
    for p in pairs:
        ci, h = p
        u_row = row(u_r[ci], cs[h])
        vtu = vt_ref[hsl[h], rows[ci]].astype(F32) * u_row
        aug = jnp.concatenate([vtu, jnp.where(sub_aug == 0, u_row, 0.0)], axis=0).astype(BF16)
        upd_sc[ci * HEADS + h] = jnp.dot(aug, k_ref[rows[ci], hsl[h]], preferred_element_type=F32)
    logw = {p: jnp.where(mask_t, row(b_r[p[0]], cs[p[1]]) + r_col[p[0]][:, cs[p[1]]:cs[p[1]] + 1], -jnp.inf)
            for p in pairs}
    mw = {p: jnp.max(logw[p], axis=0, keepdims=True) for p in pairs}
    st = {p: kq[p] * jnp.exp(logw[p] - mw[p]) for p in pairs}
    dl = {p: jnp.sum(st[p], axis=0, keepdims=True) for p in pairs}
    for p in pairs:
        ci, h = p
        a_sc[ci * HEADS + h] = jnp.dot(vt_ref[hsl[h], rows[ci]], st[p].astype(BF16),
                                       preferred_element_type=F32)
    mw_r, dl_r = {}, {}
    for ci in order:
        t_mw = jnp.zeros((8, L), F32)
        t_dl = jnp.zeros((8, L), F32)
        for h in hd:
            t_mw = jnp.where(sub8 == cs[h], mw[(ci, h)], t_mw)
            t_dl = jnp.where(sub8 == cs[h], dl[(ci, h)], t_dl)
        mw_r[ci], dl_r[ci] = t_mw, t_dl

    for ci in order:
        m_prev = m_sc[...]
        m_new = jnp.maximum(b_last[ci] + m_prev, mu[ci])
        decay = jnp.exp(b_last[ci] + m_prev - m_new)
        u_scale = jnp.exp(mu[ci] - m_new)
        m_sc[...] = m_new
        inter_log = b_r[ci] + m_prev
        m_row = jnp.maximum(inter_log, mw_r[ci])
        w_intra = jnp.exp(mw_r[ci] - m_row)
        w_cross = jnp.exp(inter_log - m_row)
        c_old = [c_sc[h] for h in hd]
        x = [lax.dot_general(c_old[h].astype(BF16), q_ref[rows[ci], hsl[h]], (((1,), (1,)), ((), ())),
                             preferred_element_type=F32) for h in hd]
        for h in hd:
            c_sc[h] = (decay[cs[h]:cs[h] + 1, 0:1] * c_old[h]
                       + u_scale[cs[h]:cs[h] + 1, 0:1] * upd_sc[ci * HEADS + h])
        dc_r = jnp.zeros((8, L), F32)
        for h in hd:
            dc_r = jnp.where(sub8 == cs[h], x[h][HEAD_DIM:HEAD_DIM + 1, :], dc_r)
        den = w_intra * dl_r[ci] + w_cross * dc_r
        rden = 1.0 / jnp.maximum(jnp.abs(den), jnp.exp(-m_row))
        f_intra = w_intra * rden
        f_cross = w_cross * rden
        ht = [row(f_intra, cs[h]) * a_sc[ci * HEADS + h] + row(f_cross, cs[h]) * x[h][0:HEAD_DIM, :] for h in hd]
        if rev:
            hsum = [hft_ref[hsl[h], rows[ci]] + ht[h] for h in hd]
            inv = [lax.rsqrt(jnp.mean(hsum[h] * hsum[h], axis=0, keepdims=True) + EPS) for h in hd]
            for h in hd:
                mt = jax.nn.sigmoid(ot_ref[hsl[h], rows[ci]]) * (hsum[h] * inv[h] * hgt_ref[hsl[h], :])
                out_ref[rows[ci], hsl[h]] = mt.T.astype(out_ref.dtype)
        else:
            for h in hd:
                out_ref[hsl[h], rows[ci]] = ht[h]


def _mlstm(q, k, vt, gt, tri, tri_t, B, S, rev, hft=None, ot=None, head_g_t=None):
    T = B * S
    nrow = CHUNK * MLSTM_CHUNKS_PER_STEP
    nc = S // nrow
    if rev:
        step = lambda b, c: b * nc + (nc - 1 - c)
    else:
        step = lambda b, c: b * nc + c
    tok = lambda b, c: (step(b, c), 0)
    tok_t = lambda b, c: (0, step(b, c))
    fix = lambda b, c: (0, 0)
    blk = lambda w: pl.BlockSpec((nrow, w), tok)
    blk_t = pl.BlockSpec((MLSTM_WIDTH, nrow), tok_t)
    in_specs = [blk(512), blk(512), blk_t, blk(LANES), pl.BlockSpec((CHUNK, CHUNK), fix),
                pl.BlockSpec((CHUNK, CHUNK), fix)]
    args = [q, k, vt, gt, tri, tri_t]
    if rev:
        in_specs += [blk_t, blk_t, pl.BlockSpec((MLSTM_WIDTH, LANES), fix)]
        args += [hft, ot, head_g_t]
        out_spec, out_shape = blk(512), jax.ShapeDtypeStruct((T, MLSTM_WIDTH), BF16)
    else:
        out_spec, out_shape = blk_t, jax.ShapeDtypeStruct((MLSTM_WIDTH, T), F32)
    return pl.pallas_call(
        functools.partial(_mlstm_kernel, rev=rev),
        grid=(B, nc),
        in_specs=in_specs,
        out_specs=out_spec,
        out_shape=out_shape,
        scratch_shapes=[pltpu.VMEM((HEADS, STATE_ROWS, HEAD_DIM), F32), pltpu.VMEM((8, LANES), F32),
                        pltpu.VMEM((MLSTM_CHUNKS_PER_STEP * HEADS, HEAD_DIM, CHUNK), F32),
                        pltpu.VMEM((MLSTM_CHUNKS_PER_STEP * HEADS, STATE_ROWS, HEAD_DIM), F32)],
        compiler_params=_cparams(2),
        name="mlstm_bwd" if rev else "mlstm_fwd",
    )(*args)


def _mix_kernel(x_ref, u_ref, up_ref, un_ref, m_ref, pw_ref, ps_ref, wo_ref, g2_ref, wr_ref, br_ref, su_ref,
                cin_ref, x1_ref, xn_ref, eid_ref, rank_ref, ew_ref, cnt_ref, ext_sc, run_sc, *, seq_len):
    tm = TM_MIX
    i = pl.program_id(0)
    tps = seq_len // tm
    it = i % tps

    @pl.when(i == 0)
    def _():
        run_sc[...] = cin_ref[...]

    ext_sc[0:POOL_HALO, :] = jnp.where(it == 0, 0.0, up_ref[...])
    ext_sc[POOL_HALO:POOL_HALO + tm, :] = u_ref[...]
    ext_sc[POOL_HALO + tm:, :] = jnp.where(it == tps - 1, 0.0, un_ref[...])
    t_seq = it * tm + lax.broadcasted_iota(I32, (tm, 1), 0)
    a_parts = []
    for gi, w in enumerate(POOL_WINDOWS):
        ls = slice(gi * POOL_GROUP, (gi + 1) * POOL_GROUP)
        acc = None
        for kk in range(-(w // 2), w // 2):
            piece = ext_sc[POOL_HALO + kk:POOL_HALO + kk + tm, ls]
            acc = piece if acc is None else acc + piece
        cnt = (jnp.minimum(t_seq + w // 2, seq_len) - jnp.maximum(t_seq - w // 2, 0)).astype(F32)
        d = acc / cnt - u_ref[:, ls]
        a_parts.append(jnp.dot(d.astype(BF16), pw_ref[gi], preferred_element_type=F32))
    a = (jnp.concatenate(a_parts, axis=1) * ps_ref[...]).astype(BF16)

    x1 = (x_ref[...]
          + jnp.dot(a, wo_ref[0:POOL_WIDTH, :], preferred_element_type=F32)
          + jnp.dot(m_ref[...], wo_ref[POOL_WIDTH:, :], preferred_element_type=F32))
    x1_ref[...] = x1
    xn = x1 * lax.rsqrt(jnp.mean(x1 * x1, axis=-1, keepdims=True) + EPS) * g2_ref[...]
    _store_row_tiles(xn_ref, xn)

    lg = lax.dot_general(wr_ref[...], xn.astype(BF16), (((1,), (1,)), ((), ())),
                         preferred_element_type=F32) + br_ref[:, 0:1]
    gl = lg[0:N_GROUPS, :]
    gmax = jnp.max(gl, axis=0, keepdims=True)
    giota = lax.broadcasted_iota(I32, gl.shape, 0).astype(F32)
    g_idx = jnp.min(jnp.where(gl == gmax, giota, float(N_GROUPS)), axis=0, keepdims=True)
    g_w = 1.0 / jnp.sum(jnp.exp(gl - gmax), axis=0, keepdims=True)
    el = lg[8:8 + EPG, :]
    for gg in range(1, N_GROUPS):
        el = jnp.where(g_idx == float(gg), lg[8 + gg * EPG:8 + (gg + 1) * EPG, :], el)
    eiota = lax.broadcasted_iota(I32, el.shape, 0).astype(F32)
    top1 = jnp.max(el, axis=0, keepdims=True)
    i1 = jnp.min(jnp.where(el == top1, eiota, float(EPG)), axis=0, keepdims=True)
    el2 = jnp.where(eiota == i1, -jnp.inf, el)
    top2 = jnp.max(el2, axis=0, keepdims=True)
    i2 = jnp.min(jnp.where(el2 == top2, eiota, float(EPG)), axis=0, keepdims=True)
    ex = jnp.exp(top2 - top1)
    w1 = g_w / (1.0 + ex)
    w2 = g_w * ex / (1.0 + ex)
    e1 = g_idx * float(EPG) + i1
    e2 = g_idx * float(EPG) + i2
    eid_ref[0:1, :] = e1.astype(I32)
    eid_ref[1:2, :] = e2.astype(I32)

    xiota = lax.broadcasted_iota(I32, (N_EXPERTS, tm), 0).astype(F32)
    oh1 = xiota == e1
    oh2 = xiota == e2
    oh = jnp.where(oh1 | oh2, 1.0, 0.0)
    before = jnp.dot(oh.astype(BF16), su_ref[...], preferred_element_type=F32)
    base = run_sc[:, 0:1] + before
    rank_ref[0:1, :] = jnp.sum(jnp.where(oh1, base, 0.0), axis=0, keepdims=True).astype(I32)
    rank_ref[1:2, :] = jnp.sum(jnp.where(oh2, base, 0.0), axis=0, keepdims=True).astype(I32)
    run_new = run_sc[...] + jnp.sum(oh, axis=1, keepdims=True)
    run_sc[...] = run_new
    cnt_ref[...] = run_new

    riota = lax.broadcasted_iota(I32, (LANES, tm), 0)
    wt = jnp.where(riota == 0, w1, jnp.where(riota == 1, w2, 0.0))
    ew_ref[...] = wt.T


def _mix(x2, u, m, pool_w, pool_scale, w_out, norm2_g, wr, br, su, counts_in, seq_len):
    T = x2.shape[0]
    tm = TM_MIX
    hb = tm // POOL_HALO
    nhb = T // POOL_HALO
    tok = lambda i: (i, 0)
    fix = lambda i: (0, 0)
    outs = (
        jax.ShapeDtypeStruct((T, D_MODEL), F32),
        jax.ShapeDtypeStruct((T * ROW_SUB, LANES), F32),
        jax.ShapeDtypeStruct((2, T), I32),
        jax.ShapeDtypeStruct((2, T), I32),
        jax.ShapeDtypeStruct((T, LANES), F32),
        jax.ShapeDtypeStruct((N_EXPERTS, LANES), F32),
    )
    return pl.pallas_call(
        functools.partial(_mix_kernel, seq_len=seq_len),
        grid=(T // tm,),
        in_specs=[
            pl.BlockSpec((tm, D_MODEL), tok),
            pl.BlockSpec((tm, POOL_WIDTH), tok),
            pl.BlockSpec((POOL_HALO, POOL_WIDTH), lambda i: (jnp.maximum(i * hb - 1, 0), 0)),
            pl.BlockSpec((POOL_HALO, POOL_WIDTH), lambda i: (jnp.minimum((i + 1) * hb, nhb - 1), 0)),
            pl.BlockSpec((tm, MLSTM_WIDTH), tok),
            pl.BlockSpec((len(POOL_WINDOWS), POOL_GROUP, POOL_GROUP), lambda i: (0, 0, 0)),
            pl.BlockSpec((1, POOL_WIDTH), fix),
            pl.BlockSpec((D_MODEL, D_MODEL), fix),
            pl.BlockSpec((1, D_MODEL), fix),
            pl.BlockSpec((ROUTER_ROWS, D_MODEL), fix),
            pl.BlockSpec((ROUTER_ROWS, LANES), fix),
            pl.BlockSpec((tm, tm), fix),
            pl.BlockSpec((N_EXPERTS, LANES), fix),
        ],
        out_specs=[
            pl.BlockSpec((tm, D_MODEL), tok),
            pl.BlockSpec((tm * ROW_SUB, LANES), tok),
            pl.BlockSpec((2, tm), lambda i: (0, i)),
            pl.BlockSpec((2, tm), lambda i: (0, i)),
            pl.BlockSpec((tm, LANES), tok),
            pl.BlockSpec((N_EXPERTS, LANES), fix),
        ],
        out_shape=outs,
        scratch_shapes=[pltpu.VMEM((tm + 2 * POOL_HALO, POOL_WIDTH), F32), pltpu.VMEM((N_EXPERTS, LANES), F32)],
        compiler_params=_cparams(1),
        name="mix_router",
    )(x2, u, u, u, m, pool_w, pool_scale, w_out, norm2_g, wr, br, su, counts_in)


def _row_copy(src, dst, sem):
    return pltpu.make_async_copy(src, dst, sem)


def _push_kernel(zs_ref, d0_ref, d1_ref, xn_ref, rows_ref, zero_sc, sem, zsem):
    tm = TM_ROWS

    @pl.when(pl.program_id(0) == 0)
    def _():
        zero_sc[...] = jnp.zeros_like(zero_sc)

        def fill(e):
            start = pl.multiple_of(jnp.maximum(zs_ref[e], 0) * ROW_SUB, ROW_SUB)
            return _row_copy(zero_sc, rows_ref.at[pl.ds(start, BM_EXPERT * ROW_SUB), :], zsem)

        for e in range(2 * N_EXPERTS):
            @pl.when(zs_ref[e] >= 0)
            def _():
                fill(e).start()
        for e in range(2 * N_EXPERTS):
            @pl.when(zs_ref[e] >= 0)
            def _():
                fill(e).wait()

    def issue(t, carry):
        src = _row_tile(xn_ref, t)
        _row_copy(src, _row_tile(rows_ref, d0_ref[0, 0, t]), sem).start(priority=0)
        _row_copy(src, _row_tile(rows_ref, d1_ref[0, 0, t]), sem).start(priority=1)
        return carry

    lax.fori_loop(0, tm, issue, 0, unroll=4)

    def drain(t, carry):
        src = _row_tile(xn_ref, t)
        _row_copy(src, _row_tile(rows_ref, d0_ref[0, 0, t]), sem).wait()
        _row_copy(src, _row_tile(rows_ref, d1_ref[0, 0, t]), sem).wait()
        return carry

    lax.fori_loop(0, tm, drain, 0, unroll=4)


def _push(zero_start, dest0, dest1, xn, n_rows):
    T = xn.shape[0] // ROW_SUB
    tm = TM_ROWS
    smem_blk = pl.BlockSpec((1, 1, tm), lambda i, zs: (i, 0, 0), memory_space=pltpu.SMEM)
    grid_spec = pltpu.PrefetchScalarGridSpec(
        num_scalar_prefetch=1,
        grid=(T // tm,),
        in_specs=[smem_blk, smem_blk, pl.BlockSpec((tm * ROW_SUB, LANES), lambda i, zs: (i, 0))],
        out_specs=pl.BlockSpec(memory_space=pl.ANY),
        scratch_shapes=[pltpu.VMEM((BM_EXPERT * ROW_SUB, LANES), F32), pltpu.SemaphoreType.DMA(()),
                        pltpu.SemaphoreType.DMA(())],
    )
    return pl.pallas_call(
        _push_kernel,
        grid_spec=grid_spec,
        out_shape=jax.ShapeDtypeStruct((n_rows * ROW_SUB, LANES), F32),
        compiler_params=_cparams(1),
        name="row_push",
    )(zero_start, dest0, dest1, xn)


def _expert_kernel(be_ref, nv_ref, x_ref, wg_ref, wu_ref, wd_ref, y_ref, wg_sc, wu_sc, wd_sc):
    i = pl.program_id(0)
    valid = i < nv_ref[0]
    new_expert = (i == 0) | (be_ref[i] != be_ref[jnp.maximum(i - 1, 0)])

    @pl.when(valid & new_expert)
    def _():
        wg_sc[...] = wg_ref[0].astype(BF16)
        wu_sc[...] = wu_ref[0].astype(BF16)
        wd_sc[...] = wd_ref[0].astype(BF16)

    @pl.when(valid)
    def _():
        xb = _load_row_tiles(x_ref).astype(BF16)
        gate = jnp.dot(xb, wg_sc[...], preferred_element_type=F32)
        up = jnp.dot(xb, wu_sc[...], preferred_element_type=F32)
        hb = (gate * jax.nn.sigmoid(gate) * up).astype(BF16)
        _store_row_tiles(y_ref, jnp.dot(hb, wd_sc[...], preferred_element_type=F32))

    @pl.when(jnp.logical_not(valid))
    def _():
        y_ref[...] = jnp.zeros_like(y_ref)


def _experts(block_e, n_valid, rows, w_gate, w_up, w_down):
    P = rows.shape[0] // ROW_SUB
    bm = BM_EXPERT
    last = lambda i, nv: jnp.minimum(i, jnp.maximum(nv[0] - 1, 0))
    row_blk = lambda i, be, nv: (last(i, nv), 0)
    w_blk = lambda i, be, nv: (be[last(i, nv)], 0, 0)
    grid_spec = pltpu.PrefetchScalarGridSpec(
        num_scalar_prefetch=2,
        grid=(P // bm,),
        in_specs=[
            pl.BlockSpec((bm * ROW_SUB, LANES), row_blk),
            pl.BlockSpec((1, D_MODEL, D_EXPERT), w_blk),
            pl.BlockSpec((1, D_MODEL, D_EXPERT), w_blk),
            pl.BlockSpec((1, D_EXPERT, D_MODEL), w_blk),
        ],
        out_specs=pl.BlockSpec((bm * ROW_SUB, LANES), lambda i, be, nv: (i, 0)),
        scratch_shapes=[pltpu.VMEM((D_MODEL, D_EXPERT), BF16), pltpu.VMEM((D_MODEL, D_EXPERT), BF16),
                        pltpu.VMEM((D_EXPERT, D_MODEL), BF16)],
    )
    return pl.pallas_call(
        _expert_kernel,
        grid_spec=grid_spec,
        out_shape=jax.ShapeDtypeStruct((P * ROW_SUB, LANES), F32),
        compiler_params=_cparams(1),
        name="experts",
    )(block_e, n_valid, rows, w_gate, w_up, w_down)


def _combine_kernel(d0_ref, d1_ref, d0n_ref, d1n_ref, x1_ref, ew_ref, gf_ref, y_ref, out_ref, buf, sem):
    tm = TM_ROWS
    i = pl.program_id(0)
    slot = i % 2

    def copies(d0, d1, sl, t):
        return (_row_copy(_row_tile(y_ref, d0[0, 0, t]), _row_tile(buf.at[sl, 0], t), sem.at[sl]),
                _row_copy(_row_tile(y_ref, d1[0, 0, t]), _row_tile(buf.at[sl, 1], t), sem.at[sl]))

    def gather_start(d0, d1, sl):
        def body(t, carry):
            c0, c1 = copies(d0, d1, sl, t)
            c0.start(priority=0)
            c1.start(priority=1)
            return carry
        lax.fori_loop(0, tm, body, 0, unroll=4)

    @pl.when(i == 0)
    def _():
        gather_start(d0_ref, d1_ref, 0)

    @pl.when(i + 1 < pl.num_programs(0))
    def _():
        gather_start(d0n_ref, d1n_ref, 1 - slot)

    def drain(t, carry):
        c0, c1 = copies(d0_ref, d1_ref, slot, t)
        c0.wait()
        c1.wait()
        return carry

    lax.fori_loop(0, tm, drain, 0, unroll=4)

    ew = ew_ref[...]
    x = (x1_ref[...] + _load_row_tiles(buf.at[slot, 0]) * ew[:, 0:1]
         + _load_row_tiles(buf.at[slot, 1]) * ew[:, 1:2])
    out_ref[...] = x * lax.rsqrt(jnp.mean(x * x, axis=-1, keepdims=True) + EPS) * gf_ref[...]


def _combine(dest0, dest1, x1, ew, final_g, yrows):
    T = x1.shape[0]
    tm = TM_ROWS
    nt = T // tm
    smem_blk = pl.BlockSpec((1, 1, tm), lambda i: (i, 0, 0), memory_space=pltpu.SMEM)
    smem_next = pl.BlockSpec((1, 1, tm), lambda i: (jnp.minimum(i + 1, nt - 1), 0, 0), memory_space=pltpu.SMEM)
    tok = lambda i: (i, 0)
    return pl.pallas_call(
        _combine_kernel,
        grid=(nt,),
        in_specs=[smem_blk, smem_blk, smem_next, smem_next,
                  pl.BlockSpec((tm, D_MODEL), tok),
                  pl.BlockSpec((tm, LANES), tok),
                  pl.BlockSpec((1, D_MODEL), lambda i: (0, 0)),
                  pl.BlockSpec(memory_space=pl.ANY)],
        out_specs=pl.BlockSpec((tm, D_MODEL), tok),
        out_shape=jax.ShapeDtypeStruct((T, D_MODEL), F32),
        scratch_shapes=[pltpu.VMEM((2, 2, tm * ROW_SUB, LANES), F32), pltpu.SemaphoreType.DMA((2,))],
        compiler_params=_cparams(1),
        name="combine",
    )(dest0, dest1, dest0, dest1, x1, ew, final_g, yrows)


def _prepare_params(norm1_g, w_in, pool_w, pool_scale, gate_b, head_norm_g, w_out, norm2_g,
                    router_group_w, router_group_b, router_expert_w, router_expert_b,
                    expert_w_gate, expert_w_up, expert_w_down, final_norm_g):
    w = w_in[0]
    n_main = 5 * 512
    w_main = w[:, :3 * 512].astype(BF16)
    w_vo_t = w[:, 3 * 512:n_main].T.astype(BF16)
    w_gate = jnp.pad(w[:, n_main:], ((0, 0), (0, LANES - N_GATES))).astype(BF16)
    gb = jnp.pad(gate_b[0], (0, LANES - N_GATES)).reshape(1, LANES)
    wr = jnp.zeros((ROUTER_ROWS, D_MODEL), F32)
    wr = wr.at[0:N_GROUPS].set(router_group_w[0].T)
    wr = wr.at[8:].set(router_expert_w[0].reshape(D_MODEL, N_EXPERTS).T)
    br = jnp.zeros((ROUTER_ROWS,), F32)
    br = br.at[0:N_GROUPS].set(router_group_b[0])
    br = br.at[8:].set(router_expert_b[0].reshape(N_EXPERTS))
    br = jnp.broadcast_to(br[:, None], (ROUTER_ROWS, LANES))
    idx = jnp.arange(CHUNK)
    tri_f = (idx[None, :] <= idx[:, None]).astype(F32)
    tri_b = (idx[None, :] >= idx[:, None]).astype(F32)
    tidx = jnp.arange(TM_MIX)
    su = (tidx[:, None] < tidx[None, :]).astype(BF16)
    return dict(
        norm1_g=norm1_g[0].reshape(1, D_MODEL), w_main=w_main, w_vo_t=w_vo_t, w_gate=w_gate, gate_b=gb,
        pool_w=pool_w[0].astype(BF16), pool_scale=pool_scale[0].reshape(1, POOL_WIDTH),
        head_g_t=jnp.broadcast_to(head_norm_g[0][:, None], (MLSTM_WIDTH, LANES)), w_out=w_out[0].astype(BF16),
        norm2_g=norm2_g[0].reshape(1, D_MODEL), wr=wr.astype(BF16), br=br, tri_f=tri_f, tri_b=tri_b, su=su,
        e_gate=expert_w_gate[0], e_up=expert_w_up[0], e_down=expert_w_down[0],
        final_g=final_norm_g.reshape(1, D_MODEL),
    )


def _encoder(x, p):
    B, S, _ = x.shape
    T = B * S
    x2 = x.reshape(T, D_MODEL)
    u, q, k, vt, ot, gt = _inproj(x2, p["norm1_g"], p["w_main"], p["w_vo_t"], p["w_gate"], p["gate_b"])
    hft = _mlstm(q, k, vt, gt, p["tri_f"], p["tri_b"], B, S, rev=False)
    m = _mlstm(q, k, vt, gt, p["tri_b"], p["tri_f"], B, S, rev=True, hft=hft, ot=ot, head_g_t=p["head_g_t"])
    counts0 = jnp.zeros((N_EXPERTS, LANES), F32)
    x1, xn, eid, rank, ew, cnt = _mix(x2, u, m, p["pool_w"], p["pool_scale"], p["w_out"], p["norm2_g"],
                                      p["wr"], p["br"], p["su"], counts0, S)

    bm = BM_EXPERT
    nb = (2 * T) // bm + N_EXPERTS
    counts = cnt[:, 0].astype(I32)
    pcounts = ((counts + bm - 1) // bm) * bm
    pends = jnp.cumsum(pcounts)
    pstarts = pends - pcounts
    e_ids = jnp.arange(N_EXPERTS, dtype=I32)[:, None, None]
    dest = rank + jnp.sum(jnp.where(eid[None] == e_ids, pstarts[:, None, None], 0), axis=0)
    block_e = jnp.minimum(jnp.sum(pends[None, :] <= (jnp.arange(nb, dtype=I32) * bm)[:, None], axis=1),
                          N_EXPERTS - 1).astype(I32)
    n_valid = (pends[-1:] // bm).astype(I32)
    last_block = jnp.where(counts > 0, pends - bm, -1)
    tail_start = pends[-1] + jnp.arange(N_EXPERTS, dtype=I32) * bm
    tail_block = jnp.where(tail_start < nb * bm, tail_start, -1)
    zero_start = jnp.concatenate([last_block, tail_block]).astype(I32)
    nt = T // TM_ROWS
    dest0 = dest[0].reshape(nt, 1, TM_ROWS)
    dest1 = dest[1].reshape(nt, 1, TM_ROWS)

    rows = _push(zero_start, dest0, dest1, xn, nb * bm)
    yrows = _experts(block_e, n_valid, rows, p["e_gate"], p["e_up"], p["e_down"])
    y = _combine(dest0, dest1, x1, ew, p["final_g"], yrows)
    return y.reshape(B, S, D_MODEL)


def kernel(x_prompt, x_sample, norm1_g, w_in, pool_w, pool_scale, gate_b, head_norm_g, w_out, norm2_g,
           router_group_w, router_group_b, router_expert_w, router_expert_b,
           expert_w_gate, expert_w_up, expert_w_down, final_norm_g):
    p = _prepare_params(norm1_g, w_in, pool_w, pool_scale, gate_b, head_norm_g, w_out, norm2_g,
                        router_group_w, router_group_b, router_expert_w, router_expert_b,
                        expert_w_gate, expert_w_up, expert_w_down, final_norm_g)
    return (_encoder(x_prompt, p), _encoder(x_sample, p))
```

```python
import functools

import jax
import jax.numpy as jnp
from jax import lax
from jax.experimental import pallas as pl
from jax.experimental.pallas import tpu as pltpu

F32 = jnp.float32
BF16 = jnp.bfloat16
I32 = jnp.int32

D_MODEL = 1024
POOL_WIDTH = 512
POOL_WINDOWS = (2, 4, 8, 16)
POOL_GROUP = 128
POOL_HALO = 16
POOL_BLOCK = 128
MLSTM_WIDTH = 512
HEADS = 4
HEAD_DIM = 128
CHUNK = 128
N_GATES = 16
N_GROUPS = 4
EPG = 8
N_EXPERTS = 32
D_EXPERT = 512
EPS = 1e-6
LANES = 128
ROUTER_ROWS = 8 + N_EXPERTS

MLSTM_CHUNKS_PER_STEP = 8
STATE_ROWS = HEAD_DIM + 16
TM_PROJ = 512
TM_MIX = 512
TM_ROWS = 256
ROW_SUB = D_MODEL // LANES
BM_EXPERT = 512
VMEM_LIMIT = 56 * 1024 * 1024

HIGHEST = lax.Precision.HIGHEST


def _cparams(n_axes):
    return pltpu.CompilerParams(dimension_semantics=("arbitrary",) * n_axes,
                                vmem_limit_bytes=VMEM_LIMIT)


def _store_row_tiles(ref, val):
    rows = val.shape[0]
    for j in range(ROW_SUB):
        ref[pl.ds(j, rows, stride=ROW_SUB), :] = val[:, j * LANES:(j + 1) * LANES]


def _load_row_tiles(ref):
    rows = ref.shape[0] // ROW_SUB
    return jnp.concatenate([ref[pl.ds(j, rows, stride=ROW_SUB), :] for j in range(ROW_SUB)], axis=1)


def _row_tile(ref, r):
    return ref.at[pl.ds(pl.multiple_of(r * ROW_SUB, ROW_SUB), ROW_SUB), :]


def _inproj_kernel(x_ref, g_ref, w_ref, wt_ref, wg_ref, gb_ref, u_ref, k_ref, qt_ref, vt_ref, ot_ref, gt_ref):
    x = x_ref[...]
    inv = lax.rsqrt(jnp.mean(x * x, axis=-1, keepdims=True) + EPS)
    xn = (x * inv * g_ref[...]).astype(BF16)

    def sec(i):
        return jnp.dot(xn, w_ref[:, i * 512:(i + 1) * 512], preferred_element_type=F32)

    def sec_t(i):
        return lax.dot_general(wt_ref[i * 512:(i + 1) * 512, :], xn, (((1,), (1,)), ((), ())),
                               preferred_element_type=F32)

    u_ref[...] = sec(0)
    k_ref[...] = (sec(1) * (HEAD_DIM ** -0.5)).astype(BF16)
    qt_ref[...] = sec_t(0).astype(BF16)
    vt_ref[...] = sec_t(1).astype(BF16)
    ot_ref[...] = sec_t(2)
    g = jnp.dot(xn, wg_ref[...], preferred_element_type=F32) + gb_ref[...]
    lane = lax.broadcasted_iota(I32, g.shape, 1)
    logsig = jnp.minimum(g, 0.0) - jnp.log1p(jnp.exp(-jnp.abs(g)))
    gt_ref[...] = jnp.where(lane >= 2 * HEADS, logsig, g)


def _inproj(x2, norm_g, w_main, w_feat_t, w_gate, gate_b):
    T = x2.shape[0]
    tm = TM_PROJ
    tok = lambda i: (i, 0)
    tok_t = lambda i: (0, i)
    fix = lambda i: (0, 0)
    outs = (
        jax.ShapeDtypeStruct((T, POOL_WIDTH), F32),
        jax.ShapeDtypeStruct((T, MLSTM_WIDTH), BF16),
        jax.ShapeDtypeStruct((MLSTM_WIDTH, T), BF16),
        jax.ShapeDtypeStruct((MLSTM_WIDTH, T), BF16),
        jax.ShapeDtypeStruct((MLSTM_WIDTH, T), F32),
        jax.ShapeDtypeStruct((T, LANES), F32),
    )
    return pl.pallas_call(
        _inproj_kernel,
        grid=(T // tm,),
        in_specs=[
            pl.BlockSpec((tm, D_MODEL), tok),
            pl.BlockSpec((1, D_MODEL), fix),
            pl.BlockSpec((D_MODEL, 2 * 512), fix),
            pl.BlockSpec((3 * 512, D_MODEL), fix),
            pl.BlockSpec((D_MODEL, LANES), fix),
            pl.BlockSpec((1, LANES), fix),
        ],
        out_specs=[
            pl.BlockSpec((tm, 512), tok), pl.BlockSpec((tm, 512), tok), pl.BlockSpec((512, tm), tok_t),
            pl.BlockSpec((512, tm), tok_t), pl.BlockSpec((512, tm), tok_t), pl.BlockSpec((tm, LANES), tok),
        ],
        out_shape=outs,
        compiler_params=_cparams(1),
        name="inproj",
    )(x2, norm_g, w_main, w_feat_t, w_gate, gate_b)


def _mlstm_kernel(*refs, rev):
    if rev:
        (k_ref, qt_ref, vt_ref, gt_ref, tri_ref, trit_ref, hft_ref, ot_ref, hgt_ref, out_ref,
         c_sc, m_sc, a_sc, upd_sc) = refs
    else:
        (k_ref, qt_ref, vt_ref, gt_ref, tri_ref, trit_ref, out_ref, c_sc, m_sc, a_sc, upd_sc) = refs
    L = CHUNK

    @pl.when(pl.program_id(1) == 0)
    def _():
        c_sc[...] = jnp.zeros_like(c_sc)
        m_sc[...] = jnp.zeros_like(m_sc)

    tri = tri_ref[...]
    mask_t = trit_ref[...] > 0.0
    sub8 = lax.broadcasted_iota(I32, (8, L), 0)
    sub_aug = lax.broadcasted_iota(I32, (STATE_ROWS - HEAD_DIM, L), 0)

    nch = MLSTM_CHUNKS_PER_STEP
    order = [nch - 1 - i if rev else i for i in range(nch)]
    hd = range(HEADS)
    cs = [HEADS * rev + h for h in hd]
    hsl = [slice(h * HEAD_DIM, (h + 1) * HEAD_DIM) for h in hd]
    rows = [slice(ci * L, (ci + 1) * L) for ci in range(nch)]
    pairs = [(ci, h) for ci in order for h in hd]
    row = lambda t, c: t[c:c + 1, :]
    last = 0 if rev else L - 1

    g = {ci: gt_ref[rows[ci], :] for ci in order}
    b_all = {ci: jnp.dot(tri, g[ci], precision=HIGHEST, preferred_element_type=F32) for ci in order}
    ba = {ci: pltpu.roll(b_all[ci], LANES - 2 * HEADS, axis=1) for ci in order}
    r_col = {ci: g[ci] - ba[ci] for ci in order}
    i_r = {ci: g[ci].T[0:8, :] for ci in order}
    b_r = {ci: ba[ci].T[0:8, :] for ci in order}
    b_last = {ci: b_r[ci][:, last:last + 1] for ci in order}
    logu = {ci: b_last[ci] - b_r[ci] + i_r[ci] for ci in order}
    mu = {ci: jnp.max(logu[ci], axis=1, keepdims=True) for ci in order}
    u_r = {ci: jnp.exp(logu[ci] - mu[ci]) for ci in order}

    kq = {p: jnp.dot(k_ref[rows[p[0]], hsl[p[1]]], qt_ref[hsl[p[1]], rows[p[0]]],
                     preferred_element_type=F32) for p in pairs}
    for p in pairs:
        ci, h = p
        u_row = row(u_r[ci], cs[h])
        vtu = vt_ref[hsl[h], rows[ci]].astype(F32) * u_row
        aug = jnp.concatenate([vtu, jnp.where(sub_aug == 0, u_row, 0.0)], axis=0).astype(BF16)
        upd_sc[ci * HEADS + h] = jnp.dot(aug, k_ref[rows[ci], hsl[h]], preferred_element_type=F32)
    logw = {p: jnp.where(mask_t, row(b_r[p[0]], cs[p[1]]) + r_col[p[0]][:, cs[p[1]]:cs[p[1]] + 1], -jnp.inf)
            for p in pairs}
    mw = {p: jnp.max(logw[p], axis=0, keepdims=True) for p in pairs}
    st = {p: kq[p] * jnp.exp(logw[p] - mw[p]) for p in pairs}
    dl = {p: jnp.sum(st[p], axis=0, keepdims=True) for p in pairs}
    for p in pairs:
        ci, h = p
        a_sc[ci * HEADS + h] = jnp.dot(vt_ref[hsl[h], rows[ci]], st[p].astype(BF16),
                                       preferred_element_type=F32)
    mw_r, dl_r = {}, {}
    for ci in order:
        t_mw = jnp.zeros((8, L), F32)
        t_dl = jnp.zeros((8, L), F32)
        for h in hd:
            t_mw = jnp.where(sub8 == cs[h], mw[(ci, h)], t_mw)
            t_dl = jnp.where(sub8 == cs[h], dl[(ci, h)], t_dl)
        mw_r[ci], dl_r[ci] = t_mw, t_dl

    for ci in order:
        m_prev = m_sc[...]
        m_new = jnp.maximum(b_last[ci] + m_prev, mu[ci])
        decay = jnp.exp(b_last[ci] + m_prev - m_new)
        u_scale = jnp.exp(mu[ci] - m_new)
        m_sc[...] = m_new
        inter_log = b_r[ci] + m_prev
        m_row = jnp.maximum(inter_log, mw_r[ci])
        w_intra = jnp.exp(mw_r[ci] - m_row)
        w_cross = jnp.exp(inter_log - m_row)
        c_old = [c_sc[h] for h in hd]
        x = [jnp.dot(c_old[h].astype(BF16), qt_ref[hsl[h], rows[ci]], preferred_element_type=F32)
             for h in hd]
        for h in hd:
            c_sc[h] = (decay[cs[h]:cs[h] + 1, 0:1] * c_old[h]
                       + u_scale[cs[h]:cs[h] + 1, 0:1] * upd_sc[ci * HEADS + h])
        dc_r = jnp.zeros((8, L), F32)
        for h in hd:
            dc_r = jnp.where(sub8 == cs[h], x[h][HEAD_DIM:HEAD_DIM + 1, :], dc_r)
        den = w_intra * dl_r[ci] + w_cross * dc_r
        rden = 1.0 / jnp.maximum(jnp.abs(den), jnp.exp(-m_row))
        f_intra = w_intra * rden
        f_cross = w_cross * rden
        ht = [row(f_intra, cs[h]) * a_sc[ci * HEADS + h] + row(f_cross, cs[h]) * x[h][0:HEAD_DIM, :] for h in hd]
        if rev:
            hsum = [hft_ref[hsl[h], rows[ci]] + ht[h] for h in hd]
            inv = [lax.rsqrt(jnp.mean(hsum[h] * hsum[h], axis=0, keepdims=True) + EPS) for h in hd]
            for h in hd:
                mt = jax.nn.sigmoid(ot_ref[hsl[h], rows[ci]]) * (hsum[h] * inv[h] * hgt_ref[hsl[h], :])
                out_ref[rows[ci], hsl[h]] = mt.T.astype(out_ref.dtype)
        else:
            for h in hd:
                out_ref[hsl[h], rows[ci]] = ht[h]


def _mlstm(k, qt, vt, gt, tri, tri_t, B, S, rev, hft=None, ot=None, head_g_t=None):
    T = B * S
    nrow = CHUNK * MLSTM_CHUNKS_PER_STEP
    nc = S // nrow
    if rev:
        step = lambda b, c: b * nc + (nc - 1 - c)
    else:
        step = lambda b, c: b * nc + c
    tok = lambda b, c: (step(b, c), 0)
    tok_t = lambda b, c: (0, step(b, c))
    fix = lambda b, c: (0, 0)
    blk = lambda w: pl.BlockSpec((nrow, w), tok)
    blk_t = pl.BlockSpec((MLSTM_WIDTH, nrow), tok_t)
    in_specs = [blk(512), blk_t, blk_t, blk(LANES), pl.BlockSpec((CHUNK, CHUNK), fix),
                pl.BlockSpec((CHUNK, CHUNK), fix)]
    args = [k, qt, vt, gt, tri, tri_t]
    if rev:
        in_specs += [blk_t, blk_t, pl.BlockSpec((MLSTM_WIDTH, LANES), fix)]
        args += [hft, ot, head_g_t]
        out_spec, out_shape = blk(512), jax.ShapeDtypeStruct((T, MLSTM_WIDTH), BF16)
    else:
        out_spec, out_shape = blk_t, jax.ShapeDtypeStruct((MLSTM_WIDTH, T), F32)
    return pl.pallas_call(
        functools.partial(_mlstm_kernel, rev=rev),
        grid=(B, nc),
        in_specs=in_specs,
        out_specs=out_spec,
        out_shape=out_shape,
        scratch_shapes=[pltpu.VMEM((HEADS, STATE_ROWS, HEAD_DIM), F32), pltpu.VMEM((8, LANES), F32),
                        pltpu.VMEM((MLSTM_CHUNKS_PER_STEP * HEADS, HEAD_DIM, CHUNK), F32),
                        pltpu.VMEM((MLSTM_CHUNKS_PER_STEP * HEADS, STATE_ROWS, HEAD_DIM), F32)],
        compiler_params=_cparams(2),
        name="mlstm_bwd" if rev else "mlstm_fwd",
    )(*args)


def _mix_kernel(x_ref, u_ref, up_ref, un_ref, m_ref, bh_ref, bl_ref, pw_ref, ps_ref, wo_ref, g2_ref, wr_ref, br_ref,
                su_ref, cin_ref, x1_ref, xn_ref, eid_ref, rank_ref, ew_ref, cnt_ref, ext_sc, run_sc, *, seq_len):
    tm = TM_MIX
    i = pl.program_id(0)
    tps = seq_len // tm
    it = i % tps

    @pl.when(i == 0)
    def _():
        run_sc[...] = cin_ref[...]

    ext_sc[0:POOL_HALO, :] = jnp.where(it == 0, 0.0, up_ref[...]).astype(BF16)
    ext_sc[POOL_HALO:POOL_HALO + tm, :] = u_ref[...].astype(BF16)
    ext_sc[POOL_HALO + tm:, :] = jnp.where(it == tps - 1, 0.0, un_ref[...]).astype(BF16)
    nblk = tm // POOL_BLOCK
    ngrp = len(POOL_WINDOWS)
    cells = [(b, gi) for b in range(nblk) for gi in range(ngrp)]
    rs = [slice(b * POOL_BLOCK, (b + 1) * POOL_BLOCK) for b in range(nblk)]
    ls = [slice(gi * POOL_GROUP, (gi + 1) * POOL_GROUP) for gi in range(ngrp)]
    first = jnp.where(it == 0, 0, 1)
    final = jnp.where(it == tps - 1, 2, 1)

    def mean(b, gi):
        slab = ext_sc[b * POOL_BLOCK:(b + 1) * POOL_BLOCK + 2 * POOL_HALO, ls[gi]]
        if 0 < b < nblk - 1:
            return jnp.dot(bh_ref[1, gi], slab, preferred_element_type=F32)
        kind = first if b == 0 else final
        return (jnp.dot(bh_ref[kind, gi], slab, preferred_element_type=F32)
                + jnp.dot(bl_ref[kind, gi], slab, preferred_element_type=F32))

    means = {c: mean(*c) for c in cells}
    dev = {c: (means[c] - u_ref[rs[c[0]], ls[c[1]]]).astype(BF16) for c in cells}
    lin = {c: jnp.dot(dev[c], pw_ref[c[1]], preferred_element_type=F32) for c in cells}
    a = jnp.concatenate(
        [(jnp.concatenate([lin[(b, gi)] for gi in range(ngrp)], axis=1) * ps_ref[...]).astype(BF16)
         for b in range(nblk)], axis=0)

    x1 = (x_ref[...]
          + jnp.dot(a, wo_ref[0:POOL_WIDTH, :], preferred_element_type=F32)
          + jnp.dot(m_ref[...], wo_ref[POOL_WIDTH:, :], preferred_element_type=F32))
    x1_ref[...] = x1
    xn = x1 * lax.rsqrt(jnp.mean(x1 * x1, axis=-1, keepdims=True) + EPS) * g2_ref[...]
    _store_row_tiles(xn_ref, xn)

    lg = lax.dot_general(wr_ref[...], xn.astype(BF16), (((1,), (1,)), ((), ())),
                         preferred_element_type=F32) + br_ref[:, 0:1]
    gl = lg[0:N_GROUPS, :]
    gmax = jnp.max(gl, axis=0, keepdims=True)
    giota = lax.broadcasted_iota(I32, gl.shape, 0).astype(F32)
    g_idx = jnp.min(jnp.where(gl == gmax, giota, float(N_GROUPS)), axis=0, keepdims=True)
    g_w = 1.0 / jnp.sum(jnp.exp(gl - gmax), axis=0, keepdims=True)
    el = lg[8:8 + EPG, :]
    for gg in range(1, N_GROUPS):
        el = jnp.where(g_idx == float(gg), lg[8 + gg * EPG:8 + (gg + 1) * EPG, :], el)
    eiota = lax.broadcasted_iota(I32, el.shape, 0).astype(F32)
    top1 = jnp.max(el, axis=0, keepdims=True)
    i1 = jnp.min(jnp.where(el == top1, eiota, float(EPG)), axis=0, keepdims=True)
    el2 = jnp.where(eiota == i1, -jnp.inf, el)
    top2 = jnp.max(el2, axis=0, keepdims=True)
    i2 = jnp.min(jnp.where(el2 == top2, eiota, float(EPG)), axis=0, keepdims=True)
    ex = jnp.exp(top2 - top1)
    w1 = g_w / (1.0 + ex)
    w2 = g_w * ex / (1.0 + ex)
    e1 = g_idx * float(EPG) + i1
    e2 = g_idx * float(EPG) + i2
    eid_ref[0:1, :] = e1.astype(I32)
    eid_ref[1:2, :] = e2.astype(I32)

    xiota = lax.broadcasted_iota(I32, (N_EXPERTS, tm), 0).astype(F32)
    oh1 = xiota == e1
    oh2 = xiota == e2
    oh = jnp.where(oh1 | oh2, 1.0, 0.0)
    before = jnp.dot(oh.astype(BF16), su_ref[...], preferred_element_type=F32)
    base = run_sc[:, 0:1] + before
    rank_ref[0:1, :] = jnp.sum(jnp.where(oh1, base, 0.0), axis=0, keepdims=True).astype(I32)
    rank_ref[1:2, :] = jnp.sum(jnp.where(oh2, base, 0.0), axis=0, keepdims=True).astype(I32)
    run_new = run_sc[...] + jnp.sum(oh, axis=1, keepdims=True)
    run_sc[...] = run_new
    cnt_ref[...] = run_new

    riota = lax.broadcasted_iota(I32, (LANES, tm), 0)
    wt = jnp.where(riota == 0, w1, jnp.where(riota == 1, w2, 0.0))
    ew_ref[...] = wt.T


def _mix(x2, u, m, band_hi, band_lo, pool_w, pool_scale, w_out, norm2_g, wr, br, su, counts_in, seq_len):
    T = x2.shape[0]
    tm = TM_MIX
    assert seq_len % tm == 0 and seq_len >= 2 * POOL_BLOCK
    hb = tm // POOL_HALO
    nhb = T // POOL_HALO
    band_spec = pl.BlockSpec((3, len(POOL_WINDOWS), POOL_BLOCK, POOL_BLOCK + 2 * POOL_HALO), lambda i: (0, 0, 0, 0))
    tok = lambda i: (i, 0)
    fix = lambda i: (0, 0)
    outs = (
        jax.ShapeDtypeStruct((T, D_MODEL), F32),
        jax.ShapeDtypeStruct((T * ROW_SUB, LANES), F32),
        jax.ShapeDtypeStruct((2, T), I32),
        jax.ShapeDtypeStruct((2, T), I32),
        jax.ShapeDtypeStruct((T, LANES), F32),
        jax.ShapeDtypeStruct((N_EXPERTS, LANES), F32),
    )
    return pl.pallas_call(
        functools.partial(_mix_kernel, seq_len=seq_len),
        grid=(T // tm,),
        in_specs=[
            pl.BlockSpec((tm, D_MODEL), tok),
            pl.BlockSpec((tm, POOL_WIDTH), tok),
            pl.BlockSpec((POOL_HALO, POOL_WIDTH), lambda i: (jnp.maximum(i * hb - 1, 0), 0)),
            pl.BlockSpec((POOL_HALO, POOL_WIDTH), lambda i: (jnp.minimum((i + 1) * hb, nhb - 1), 0)),
            pl.BlockSpec((tm, MLSTM_WIDTH), tok),
            band_spec,
            band_spec,
            pl.BlockSpec((len(POOL_WINDOWS), POOL_GROUP, POOL_GROUP), lambda i: (0, 0, 0)),
            pl.BlockSpec((1, POOL_WIDTH), fix),
            pl.BlockSpec((D_MODEL, D_MODEL), fix),
            pl.BlockSpec((1, D_MODEL), fix),
            pl.BlockSpec((ROUTER_ROWS, D_MODEL), fix),
            pl.BlockSpec((ROUTER_ROWS, LANES), fix),
            pl.BlockSpec((tm, tm), fix),
            pl.BlockSpec((N_EXPERTS, LANES), fix),
        ],
        out_specs=[
            pl.BlockSpec((tm, D_MODEL), tok),
            pl.BlockSpec((tm * ROW_SUB, LANES), tok),
            pl.BlockSpec((2, tm), lambda i: (0, i)),
            pl.BlockSpec((2, tm), lambda i: (0, i)),
            pl.BlockSpec((tm, LANES), tok),
            pl.BlockSpec((N_EXPERTS, LANES), fix),
        ],
        out_shape=outs,
        scratch_shapes=[pltpu.VMEM((tm + 2 * POOL_HALO, POOL_WIDTH), BF16), pltpu.VMEM((N_EXPERTS, LANES), F32)],
        compiler_params=_cparams(1),
        name="mix_router",
    )(x2, u, u, u, m, band_hi, band_lo, pool_w, pool_scale, w_out, norm2_g, wr, br, su, counts_in)


def _row_copy(src, dst, sem):
    return pltpu.make_async_copy(src, dst, sem)


def _push_kernel(zs_ref, d0_ref, d1_ref, xn_ref, rows_ref, zero_sc, sem, zsem):
    tm = TM_ROWS

    @pl.when(pl.program_id(0) == 0)
    def _():
        zero_sc[...] = jnp.zeros_like(zero_sc)

        def fill(e):
            start = pl.multiple_of(jnp.maximum(zs_ref[e], 0) * ROW_SUB, ROW_SUB)
            return _row_copy(zero_sc, rows_ref.at[pl.ds(start, BM_EXPERT * ROW_SUB), :], zsem)

        for e in range(2 * N_EXPERTS):
            @pl.when(zs_ref[e] >= 0)
            def _():
                fill(e).start()
        for e in range(2 * N_EXPERTS):
            @pl.when(zs_ref[e] >= 0)
            def _():
                fill(e).wait()

    def issue(t, carry):
        src = _row_tile(xn_ref, t)
        _row_copy(src, _row_tile(rows_ref, d0_ref[0, 0, t]), sem).start(priority=0)
        _row_copy(src, _row_tile(rows_ref, d1_ref[0, 0, t]), sem).start(priority=1)
        return carry

    lax.fori_loop(0, tm, issue, 0, unroll=4)

    def drain(t, carry):
        src = _row_tile(xn_ref, t)
        _row_copy(src, _row_tile(rows_ref, d0_ref[0, 0, t]), sem).wait()
        _row_copy(src, _row_tile(rows_ref, d1_ref[0, 0, t]), sem).wait()
        return carry

    lax.fori_loop(0, tm, drain, 0, unroll=4)


def _push(zero_start, dest0, dest1, xn, n_rows):
    T = xn.shape[0] // ROW_SUB
    tm = TM_ROWS
    smem_blk = pl.BlockSpec((1, 1, tm), lambda i, zs: (i, 0, 0), memory_space=pltpu.SMEM)
    grid_spec = pltpu.PrefetchScalarGridSpec(
        num_scalar_prefetch=1,
        grid=(T // tm,),
        in_specs=[smem_blk, smem_blk, pl.BlockSpec((tm * ROW_SUB, LANES), lambda i, zs: (i, 0))],
        out_specs=pl.BlockSpec(memory_space=pl.ANY),
        scratch_shapes=[pltpu.VMEM((BM_EXPERT * ROW_SUB, LANES), F32), pltpu.SemaphoreType.DMA(()),
                        pltpu.SemaphoreType.DMA(())],
    )
    return pl.pallas_call(
        _push_kernel,
        grid_spec=grid_spec,
        out_shape=jax.ShapeDtypeStruct((n_rows * ROW_SUB, LANES), F32),
        compiler_params=_cparams(1),
        name="row_push",
    )(zero_start, dest0, dest1, xn)


def _expert_kernel(be_ref, nv_ref, x_ref, wg_ref, wu_ref, wd_ref, y_ref, wg_sc, wu_sc, wd_sc):
    i = pl.program_id(0)
    valid = i < nv_ref[0]
    new_expert = (i == 0) | (be_ref[i] != be_ref[jnp.maximum(i - 1, 0)])

    @pl.when(valid & new_expert)
    def _():
        wg_sc[...] = wg_ref[0].astype(BF16)
        wu_sc[...] = wu_ref[0].astype(BF16)
        wd_sc[...] = wd_ref[0].astype(BF16)

    @pl.when(valid)
    def _():
        xb = _load_row_tiles(x_ref).astype(BF16)
        gate = jnp.dot(xb, wg_sc[...], preferred_element_type=F32)
        up = jnp.dot(xb, wu_sc[...], preferred_element_type=F32)
        hb = (gate * jax.nn.sigmoid(gate) * up).astype(BF16)
        _store_row_tiles(y_ref, jnp.dot(hb, wd_sc[...], preferred_element_type=F32))

    @pl.when(jnp.logical_not(valid))
    def _():
        y_ref[...] = jnp.zeros_like(y_ref)


def _experts(block_e, n_valid, rows, w_gate, w_up, w_down):
    P = rows.shape[0] // ROW_SUB
    bm = BM_EXPERT
    last = lambda i, nv: jnp.minimum(i, jnp.maximum(nv[0] - 1, 0))
    row_blk = lambda i, be, nv: (last(i, nv), 0)
    w_blk = lambda i, be, nv: (be[last(i, nv)], 0, 0)
    grid_spec = pltpu.PrefetchScalarGridSpec(
        num_scalar_prefetch=2,
        grid=(P // bm,),
        in_specs=[
            pl.BlockSpec((bm * ROW_SUB, LANES), row_blk),
            pl.BlockSpec((1, D_MODEL, D_EXPERT), w_blk),
            pl.BlockSpec((1, D_MODEL, D_EXPERT), w_blk),
            pl.BlockSpec((1, D_EXPERT, D_MODEL), w_blk),
        ],
        out_specs=pl.BlockSpec((bm * ROW_SUB, LANES), lambda i, be, nv: (i, 0)),
        scratch_shapes=[pltpu.VMEM((D_MODEL, D_EXPERT), BF16), pltpu.VMEM((D_MODEL, D_EXPERT), BF16),
                        pltpu.VMEM((D_EXPERT, D_MODEL), BF16)],
    )
    return pl.pallas_call(
        _expert_kernel,
        grid_spec=grid_spec,
        out_shape=jax.ShapeDtypeStruct((P * ROW_SUB, LANES), F32),
        compiler_params=_cparams(1),
        name="experts",
    )(block_e, n_valid, rows, w_gate, w_up, w_down)


def _combine_kernel(d0_ref, d1_ref, d0n_ref, d1n_ref, x1_ref, ew_ref, gf_ref, y_ref, out_ref, buf, sem):
    tm = TM_ROWS
    i = pl.program_id(0)
    slot = i % 2

    def copies(d0, d1, sl, t):
        return (_row_copy(_row_tile(y_ref, d0[0, 0, t]), _row_tile(buf.at[sl, 0], t), sem.at[sl]),
                _row_copy(_row_tile(y_ref, d1[0, 0, t]), _row_tile(buf.at[sl, 1], t), sem.at[sl]))

    def gather_start(d0, d1, sl):
        def body(t, carry):
            c0, c1 = copies(d0, d1, sl, t)
            c0.start(priority=0)
            c1.start(priority=1)
            return carry
        lax.fori_loop(0, tm, body, 0, unroll=4)

    @pl.when(i == 0)
    def _():
        gather_start(d0_ref, d1_ref, 0)

    @pl.when(i + 1 < pl.num_programs(0))
    def _():
        gather_start(d0n_ref, d1n_ref, 1 - slot)

    def drain(t, carry):
        c0, c1 = copies(d0_ref, d1_ref, slot, t)
        c0.wait()
        c1.wait()
        return carry

    lax.fori_loop(0, tm, drain, 0, unroll=4)

    ew = ew_ref[...]
    x = (x1_ref[...] + _load_row_tiles(buf.at[slot, 0]) * ew[:, 0:1]
         + _load_row_tiles(buf.at[slot, 1]) * ew[:, 1:2])
    out_ref[...] = x * lax.rsqrt(jnp.mean(x * x, axis=-1, keepdims=True) + EPS) * gf_ref[...]


def _combine(dest0, dest1, x1, ew, final_g, yrows):
    T = x1.shape[0]
    tm = TM_ROWS
    nt = T // tm
    smem_blk = pl.BlockSpec((1, 1, tm), lambda i: (i, 0, 0), memory_space=pltpu.SMEM)
    smem_next = pl.BlockSpec((1, 1, tm), lambda i: (jnp.minimum(i + 1, nt - 1), 0, 0), memory_space=pltpu.SMEM)
    tok = lambda i: (i, 0)
    return pl.pallas_call(
        _combine_kernel,
        grid=(nt,),
        in_specs=[smem_blk, smem_blk, smem_next, smem_next,
                  pl.BlockSpec((tm, D_MODEL), tok),
                  pl.BlockSpec((tm, LANES), tok),
                  pl.BlockSpec((1, D_MODEL), lambda i: (0, 0)),
                  pl.BlockSpec(memory_space=pl.ANY)],
        out_specs=pl.BlockSpec((tm, D_MODEL), tok),
        out_shape=jax.ShapeDtypeStruct((T, D_MODEL), F32),
        scratch_shapes=[pltpu.VMEM((2, 2, tm * ROW_SUB, LANES), F32), pltpu.SemaphoreType.DMA((2,))],
        compiler_params=_cparams(1),
        name="combine",
    )(dest0, dest1, dest0, dest1, x1, ew, final_g, yrows)


def _pool_bands():
    t = jnp.arange(POOL_BLOCK)[:, None]
    e = jnp.arange(POOL_BLOCK + 2 * POOL_HALO)[None, :]
    bands = []
    for kind in range(3):
        per_w = []
        for w in POOL_WINDOWS:
            half = w // 2
            win = (e >= t + POOL_HALO - half) & (e < t + POOL_HALO + half)
            lo_edge = jnp.maximum(t - half, 0) if kind == 0 else t - half
            hi_edge = jnp.minimum(t + half, POOL_BLOCK) if kind == 2 else t + half
            per_w.append(jnp.where(win, 1.0 / (hi_edge - lo_edge).astype(F32), 0.0))
        bands.append(jnp.stack(per_w))
    band = jnp.stack(bands)
    hi = band.astype(BF16)
    return hi, (band - hi.astype(F32)).astype(BF16)


def _prepare_params(norm1_g, w_in, pool_w, pool_scale, gate_b, head_norm_g, w_out, norm2_g,
                    router_group_w, router_group_b, router_expert_w, router_expert_b,
                    expert_w_gate, expert_w_up, expert_w_down, final_norm_g):
    w = w_in[0]
    n_main = 5 * 512
    w_main = jnp.concatenate([w[:, 0:512], w[:, 1024:1536]], axis=1).astype(BF16)
    w_feat_t = jnp.concatenate([w[:, 512:1024], w[:, 1536:n_main]], axis=1).T.astype(BF16)
    w_gate = jnp.pad(w[:, n_main:], ((0, 0), (0, LANES - N_GATES))).astype(BF16)
    gb = jnp.pad(gate_b[0], (0, LANES - N_GATES)).reshape(1, LANES)
    wr = jnp.zeros((ROUTER_ROWS, D_MODEL), F32)
    wr = wr.at[0:N_GROUPS].set(router_group_w[0].T)
    wr = wr.at[8:].set(router_expert_w[0].reshape(D_MODEL, N_EXPERTS).T)
    br = jnp.zeros((ROUTER_ROWS,), F32)
    br = br.at[0:N_GROUPS].set(router_group_b[0])
    br = br.at[8:].set(router_expert_b[0].reshape(N_EXPERTS))
    br = jnp.broadcast_to(br[:, None], (ROUTER_ROWS, LANES))
    idx = jnp.arange(CHUNK)
    tri_f = (idx[None, :] <= idx[:, None]).astype(F32)
    tri_b = (idx[None, :] >= idx[:, None]).astype(F32)
    tidx = jnp.arange(TM_MIX)
    su = (tidx[:, None] < tidx[None, :]).astype(BF16)
    band_hi, band_lo = _pool_bands()
    return dict(
        norm1_g=norm1_g[0].reshape(1, D_MODEL), w_main=w_main, w_feat_t=w_feat_t, w_gate=w_gate, gate_b=gb,
        pool_w=pool_w[0].astype(BF16), pool_scale=pool_scale[0].reshape(1, POOL_WIDTH),
        head_g_t=jnp.broadcast_to(head_norm_g[0][:, None], (MLSTM_WIDTH, LANES)), w_out=w_out[0].astype(BF16),
        norm2_g=norm2_g[0].reshape(1, D_MODEL), wr=wr.astype(BF16), br=br, tri_f=tri_f, tri_b=tri_b, su=su, band_hi=band_hi, band_lo=band_lo,
        e_gate=expert_w_gate[0], e_up=expert_w_up[0], e_down=expert_w_down[0],
        final_g=final_norm_g.reshape(1, D_MODEL),
    )


def _encoder(x, p):
    B, S, _ = x.shape
    T = B * S
    x2 = x.reshape(T, D_MODEL)
    u, k, qt, vt, ot, gt = _inproj(x2, p["norm1_g"], p["w_main"], p["w_feat_t"], p["w_gate"], p["gate_b"])
    hft = _mlstm(k, qt, vt, gt, p["tri_f"], p["tri_b"], B, S, rev=False)
    m = _mlstm(k, qt, vt, gt, p["tri_b"], p["tri_f"], B, S, rev=True, hft=hft, ot=ot, head_g_t=p["head_g_t"])
    counts0 = jnp.zeros((N_EXPERTS, LANES), F32)
    x1, xn, eid, rank, ew, cnt = _mix(x2, u, m, p["band_hi"], p["band_lo"], p["pool_w"], p["pool_scale"], p["w_out"], p["norm2_g"],
                                      p["wr"], p["br"], p["su"], counts0, S)

    bm = BM_EXPERT
    nb = (2 * T) // bm + N_EXPERTS
    counts = cnt[:, 0].astype(I32)
    pcounts = ((counts + bm - 1) // bm) * bm
    pends = jnp.cumsum(pcounts)
    pstarts = pends - pcounts
    e_ids = jnp.arange(N_EXPERTS, dtype=I32)[:, None, None]
    dest = rank + jnp.sum(jnp.where(eid[None] == e_ids, pstarts[:, None, None], 0), axis=0)
    block_e = jnp.minimum(jnp.sum(pends[None, :] <= (jnp.arange(nb, dtype=I32) * bm)[:, None], axis=1),
                          N_EXPERTS - 1).astype(I32)
    n_valid = (pends[-1:] // bm).astype(I32)
    last_block = jnp.where(counts > 0, pends - bm, -1)
    tail_start = pends[-1] + jnp.arange(N_EXPERTS, dtype=I32) * bm
    tail_block = jnp.where(tail_start < nb * bm, tail_start, -1)
    zero_start = jnp.concatenate([last_block, tail_block]).astype(I32)
    nt = T // TM_ROWS
    dest0 = dest[0].reshape(nt, 1, TM_ROWS)
    dest1 = dest[1].reshape(nt, 1, TM_ROWS)

    rows = _push(zero_start, dest0, dest1, xn, nb * bm)
    yrows = _experts(block_e, n_valid, rows, p["e_gate"], p["e_up"], p["e_down"])
    y = _combine(dest0, dest1, x1, ew, p["final_g"], yrows)
    return y.reshape(B, S, D_MODEL)


def kernel(x_prompt, x_sample, norm1_g, w_in, pool_w, pool_scale, gate_b, head_norm_g, w_out, norm2_g,
           router_group_w, router_group_b, router_expert_w, router_expert_b,
           expert_w_gate, expert_w_up, expert_w_down, final_norm_g):
    p = _prepare_params(norm1_g, w_in, pool_w, pool_scale, gate_b, head_norm_g, w_out, norm2_g,
                        router_group_w, router_group_b, router_expert_w, router_expert_b,
                        expert_w_gate, expert_w_up, expert_w_down, final_norm_g)
    return (_encoder(x_prompt, p), _encoder(x_sample, p))
```

```python
import functools

import jax
import jax.numpy as jnp
from jax import lax
from jax.experimental import pallas as pl
from jax.experimental.pallas import tpu as pltpu

F32 = jnp.float32
BF16 = jnp.bfloat16
I32 = jnp.int32

D_MODEL = 1024
POOL_WIDTH = 512
POOL_WINDOWS = (2, 4, 8, 16)
POOL_GROUP = 128
POOL_HALO = 16
POOL_BLOCK = 128
MLSTM_WIDTH = 512
HEADS = 4
HEAD_DIM = 128
CHUNK = 128
N_GATES = 16
N_GROUPS = 4
EPG = 8
N_EXPERTS = 32
D_EXPERT = 512
EPS = 1e-6
LANES = 128
ROUTER_ROWS = 8 + N_EXPERTS

MLSTM_CHUNKS_PER_STEP = 8
STATE_ROWS = HEAD_DIM + 16
TM_PROJ = 1024
TM_MIX = 512
TM_ROWS = 256
ROW_SUB = D_MODEL // LANES
BM_EXPERT = 512
VMEM_LIMIT = 56 * 1024 * 1024

HIGHEST = lax.Precision.HIGHEST


def _cparams(n_axes):
    return pltpu.CompilerParams(dimension_semantics=("arbitrary",) * n_axes,
                                vmem_limit_bytes=VMEM_LIMIT)


def _store_row_tiles(ref, val):
    rows = val.shape[0]
    for j in range(ROW_SUB):
        ref[pl.ds(j, rows, stride=ROW_SUB), :] = val[:, j * LANES:(j + 1) * LANES]


def _load_row_tiles(ref):
    rows = ref.shape[0] // ROW_SUB
    return jnp.concatenate([ref[pl.ds(j, rows, stride=ROW_SUB), :] for j in range(ROW_SUB)], axis=1)


def _row_tile(ref, r):
    return ref.at[pl.ds(pl.multiple_of(r * ROW_SUB, ROW_SUB), ROW_SUB), :]


def _inproj_kernel(x_ref, g_ref, w_ref, wt_ref, wg_ref, gb_ref, u_ref, k_ref, qt_ref, vt_ref, ot_ref, gt_ref):
    x = x_ref[...]
    inv = lax.rsqrt(jnp.mean(x * x, axis=-1, keepdims=True) + EPS)
    xn = (x * inv * g_ref[...]).astype(BF16)

    def sec(i):
        return jnp.dot(xn, w_ref[:, i * 512:(i + 1) * 512], preferred_element_type=F32)

    def sec_t(i):
        return lax.dot_general(wt_ref[i * 512:(i + 1) * 512, :], xn, (((1,), (1,)), ((), ())),
                               preferred_element_type=F32)

    u_ref[...] = sec(0)
    k_ref[...] = (sec(1) * (HEAD_DIM ** -0.5)).astype(BF16)
    qt_ref[...] = sec_t(0).astype(BF16)
    vt_ref[...] = sec_t(1).astype(BF16)
    ot_ref[...] = sec_t(2)
    g = jnp.dot(xn, wg_ref[...], preferred_element_type=F32) + gb_ref[...]
    lane = lax.broadcasted_iota(I32, g.shape, 1)
    logsig = jnp.minimum(g, 0.0) - jnp.log1p(jnp.exp(-jnp.abs(g)))
    gt_ref[...] = jnp.where(lane >= 2 * HEADS, logsig, g)


def _inproj(x2, norm_g, w_main, w_feat_t, w_gate, gate_b):
    T = x2.shape[0]
    tm = TM_PROJ
    tok = lambda i: (i, 0)
    tok_t = lambda i: (0, i)
    fix = lambda i: (0, 0)
    outs = (
        jax.ShapeDtypeStruct((T, POOL_WIDTH), F32),
        jax.ShapeDtypeStruct((T, MLSTM_WIDTH), BF16),
        jax.ShapeDtypeStruct((MLSTM_WIDTH, T), BF16),
        jax.ShapeDtypeStruct((MLSTM_WIDTH, T), BF16),
        jax.ShapeDtypeStruct((MLSTM_WIDTH, T), F32),
        jax.ShapeDtypeStruct((T, LANES), F32),
    )
    return pl.pallas_call(
        _inproj_kernel,
        grid=(T // tm,),
        in_specs=[
            pl.BlockSpec((tm, D_MODEL), tok),
            pl.BlockSpec((1, D_MODEL), fix),
            pl.BlockSpec((D_MODEL, 2 * 512), fix),
            pl.BlockSpec((3 * 512, D_MODEL), fix),
            pl.BlockSpec((D_MODEL, LANES), fix),
            pl.BlockSpec((1, LANES), fix),
        ],
        out_specs=[
            pl.BlockSpec((tm, 512), tok), pl.BlockSpec((tm, 512), tok), pl.BlockSpec((512, tm), tok_t),
            pl.BlockSpec((512, tm), tok_t), pl.BlockSpec((512, tm), tok_t), pl.BlockSpec((tm, LANES), tok),
        ],
        out_shape=outs,
        compiler_params=_cparams(1),
        name="inproj",
    )(x2, norm_g, w_main, w_feat_t, w_gate, gate_b)


def _mlstm_kernel(*refs, rev):
    if rev:
        (k_ref, qt_ref, vt_ref, gt_ref, tri_ref, trit_ref, hft_ref, ot_ref, hgt_ref, out_ref,
         c_sc, m_sc, a_sc, upd_sc) = refs
    else:
        (k_ref, qt_ref, vt_ref, gt_ref, tri_ref, trit_ref, out_ref, c_sc, m_sc, a_sc, upd_sc) = refs
    L = CHUNK

    @pl.when(pl.program_id(1) == 0)
    def _():
        c_sc[...] = jnp.zeros_like(c_sc)
        m_sc[...] = jnp.zeros_like(m_sc)

    tri = tri_ref[...]
    mask_t = trit_ref[...] > 0.0
    sub8 = lax.broadcasted_iota(I32, (8, L), 0)
    sub_aug = lax.broadcasted_iota(I32, (STATE_ROWS - HEAD_DIM, L), 0)

    nch = MLSTM_CHUNKS_PER_STEP
    order = [nch - 1 - i if rev else i for i in range(nch)]
    hd = range(HEADS)
    cs = [HEADS * rev + h for h in hd]
    hsl = [slice(h * HEAD_DIM, (h + 1) * HEAD_DIM) for h in hd]
    rows = [slice(ci * L, (ci + 1) * L) for ci in range(nch)]
    pairs = [(ci, h) for ci in order for h in hd]
    row = lambda t, c: t[c:c + 1, :]
    last = 0 if rev else L - 1

    g = {ci: gt_ref[rows[ci], :] for ci in order}
    b_all = {ci: jnp.dot(tri, g[ci], precision=HIGHEST, preferred_element_type=F32) for ci in order}
    ba = {ci: pltpu.roll(b_all[ci], LANES - 2 * HEADS, axis=1) for ci in order}
    r_col = {ci: g[ci] - ba[ci] for ci in order}
    i_r = {ci: g[ci].T[0:8, :] for ci in order}
    b_r = {ci: ba[ci].T[0:8, :] for ci in order}
    b_last = {ci: b_r[ci][:, last:last + 1] for ci in order}
    logu = {ci: b_last[ci] - b_r[ci] + i_r[ci] for ci in order}
    mu = {ci: jnp.max(logu[ci], axis=1, keepdims=True) for ci in order}
    u_r = {ci: jnp.exp(logu[ci] - mu[ci]) for ci in order}

    kq = {p: jnp.dot(k_ref[rows[p[0]], hsl[p[1]]], qt_ref[hsl[p[1]], rows[p[0]]],
                     preferred_element_type=F32) for p in pairs}
    for p in pairs:
        ci, h = p
        u_row = row(u_r[ci], cs[h])
        vtu = vt_ref[hsl[h], rows[ci]].astype(F32) * u_row
        aug = jnp.concatenate([vtu, jnp.where(sub_aug == 0, u_row, 0.0)], axis=0).astype(BF16)
        upd_sc[ci * HEADS + h] = jnp.dot(aug, k_ref[rows[ci], hsl[h]], preferred_element_type=F32)
    logw = {p: jnp.where(mask_t, row(b_r[p[0]], cs[p[1]]) + r_col[p[0]][:, cs[p[1]]:cs[p[1]] + 1], -jnp.inf)
            for p in pairs}
    mw = {p: jnp.max(logw[p], axis=0, keepdims=True) for p in pairs}
    st = {p: kq[p] * jnp.exp(logw[p] - mw[p]) for p in pairs}
    dl = {p: jnp.sum(st[p], axis=0, keepdims=True) for p in pairs}
    for p in pairs:
        ci, h = p
        a_sc[ci * HEADS + h] = jnp.dot(vt_ref[hsl[h], rows[ci]], st[p].astype(BF16),
                                       preferred_element_type=F32)
    mw_r, dl_r = {}, {}
    for ci in order:
        t_mw = jnp.zeros((8, L), F32)
        t_dl = jnp.zeros((8, L), F32)
        for h in hd:
            t_mw = jnp.where(sub8 == cs[h], mw[(ci, h)], t_mw)
            t_dl = jnp.where(sub8 == cs[h], dl[(ci, h)], t_dl)
        mw_r[ci], dl_r[ci] = t_mw, t_dl

    for ci in order:
        m_prev = m_sc[...]
        m_new = jnp.maximum(b_last[ci] + m_prev, mu[ci])
        decay = jnp.exp(b_last[ci] + m_prev - m_new)
        u_scale = jnp.exp(mu[ci] - m_new)
        m_sc[...] = m_new
        inter_log = b_r[ci] + m_prev
        m_row = jnp.maximum(inter_log, mw_r[ci])
        w_intra = jnp.exp(mw_r[ci] - m_row)
        w_cross = jnp.exp(inter_log - m_row)
        c_old = [c_sc[h] for h in hd]
        x = [jnp.dot(c_old[h].astype(BF16), qt_ref[hsl[h], rows[ci]], preferred_element_type=F32)
             for h in hd]
        for h in hd:
            c_sc[h] = (decay[cs[h]:cs[h] + 1, 0:1] * c_old[h]
                       + u_scale[cs[h]:cs[h] + 1, 0:1] * upd_sc[ci * HEADS + h])
        dc_r = jnp.zeros((8, L), F32)
        for h in hd:
            dc_r = jnp.where(sub8 == cs[h], x[h][HEAD_DIM:HEAD_DIM + 1, :], dc_r)
        den = w_intra * dl_r[ci] + w_cross * dc_r
        rden = 1.0 / jnp.maximum(jnp.abs(den), jnp.exp(-m_row))
        f_intra = w_intra * rden
        f_cross = w_cross * rden
        ht = [row(f_intra, cs[h]) * a_sc[ci * HEADS + h] + row(f_cross, cs[h]) * x[h][0:HEAD_DIM, :] for h in hd]
        if rev:
            hsum = [hft_ref[hsl[h], rows[ci]] + ht[h] for h in hd]
            inv = [lax.rsqrt(jnp.mean(hsum[h] * hsum[h], axis=0, keepdims=True) + EPS) for h in hd]
            for h in hd:
                mt = jax.nn.sigmoid(ot_ref[hsl[h], rows[ci]]) * (hsum[h] * inv[h] * hgt_ref[hsl[h], :])
                out_ref[rows[ci], hsl[h]] = mt.T.astype(out_ref.dtype)
        else:
            for h in hd:
                out_ref[hsl[h], rows[ci]] = ht[h]


def _mlstm(k, qt, vt, gt, tri, tri_t, B, S, rev, hft=None, ot=None, head_g_t=None):
    T = B * S
    nrow = CHUNK * MLSTM_CHUNKS_PER_STEP
    nc = S // nrow
    if rev:
        step = lambda b, c: b * nc + (nc - 1 - c)
    else:
        step = lambda b, c: b * nc + c
    tok = lambda b, c: (step(b, c), 0)
    tok_t = lambda b, c: (0, step(b, c))
    fix = lambda b, c: (0, 0)
    blk = lambda w: pl.BlockSpec((nrow, w), tok)
    blk_t = pl.BlockSpec((MLSTM_WIDTH, nrow), tok_t)
    in_specs = [blk(512), blk_t, blk_t, blk(LANES), pl.BlockSpec((CHUNK, CHUNK), fix),
                pl.BlockSpec((CHUNK, CHUNK), fix)]
    args = [k, qt, vt, gt, tri, tri_t]
    if rev:
        in_specs += [blk_t, blk_t, pl.BlockSpec((MLSTM_WIDTH, LANES), fix)]
        args += [hft, ot, head_g_t]
        out_spec, out_shape = blk(512), jax.ShapeDtypeStruct((T, MLSTM_WIDTH), BF16)
    else:
        out_spec, out_shape = blk_t, jax.ShapeDtypeStruct((MLSTM_WIDTH, T), F32)
    return pl.pallas_call(
        functools.partial(_mlstm_kernel, rev=rev),
        grid=(B, nc),
        in_specs=in_specs,
        out_specs=out_spec,
        out_shape=out_shape,
        scratch_shapes=[pltpu.VMEM((HEADS, STATE_ROWS, HEAD_DIM), F32), pltpu.VMEM((8, LANES), F32),
                        pltpu.VMEM((MLSTM_CHUNKS_PER_STEP * HEADS, HEAD_DIM, CHUNK), F32),
                        pltpu.VMEM((MLSTM_CHUNKS_PER_STEP * HEADS, STATE_ROWS, HEAD_DIM), F32)],
        compiler_params=_cparams(2),
        name="mlstm_bwd" if rev else "mlstm_fwd",
    )(*args)


def _mix_kernel(x_ref, u_ref, up_ref, un_ref, m_ref, bh_ref, bl_ref, pw_ref, ps_ref, wo_ref, g2_ref, wr_ref, br_ref,
                su_ref, cin_ref, x1_ref, xn_ref, eid_ref, rank_ref, ew_ref, cnt_ref, ext_sc, run_sc, *, seq_len):
    tm = TM_MIX
    i = pl.program_id(0)
    tps = seq_len // tm
    it = i % tps

    @pl.when(i == 0)
    def _():
        run_sc[...] = cin_ref[...]

    ext_sc[0:POOL_HALO, :] = jnp.where(it == 0, 0.0, up_ref[...]).astype(BF16)
    ext_sc[POOL_HALO:POOL_HALO + tm, :] = u_ref[...].astype(BF16)
    ext_sc[POOL_HALO + tm:, :] = jnp.where(it == tps - 1, 0.0, un_ref[...]).astype(BF16)
    nblk = tm // POOL_BLOCK
    ngrp = len(POOL_WINDOWS)
    cells = [(b, gi) for b in range(nblk) for gi in range(ngrp)]
    rs = [slice(b * POOL_BLOCK, (b + 1) * POOL_BLOCK) for b in range(nblk)]
    ls = [slice(gi * POOL_GROUP, (gi + 1) * POOL_GROUP) for gi in range(ngrp)]
    first = jnp.where(it == 0, 0, 1)
    final = jnp.where(it == tps - 1, 2, 1)

    def mean(b, gi):
        slab = ext_sc[b * POOL_BLOCK:(b + 1) * POOL_BLOCK + 2 * POOL_HALO, ls[gi]]
        if 0 < b < nblk - 1:
            return jnp.dot(bh_ref[1, gi], slab, preferred_element_type=F32)
        kind = first if b == 0 else final
        return (jnp.dot(bh_ref[kind, gi], slab, preferred_element_type=F32)
                + jnp.dot(bl_ref[kind, gi], slab, preferred_element_type=F32))

    means = {c: mean(*c) for c in cells}
    dev = {c: (means[c] - u_ref[rs[c[0]], ls[c[1]]]).astype(BF16) for c in cells}
    lin = {c: jnp.dot(dev[c], pw_ref[c[1]], preferred_element_type=F32) for c in cells}
    a = jnp.concatenate(
        [(jnp.concatenate([lin[(b, gi)] for gi in range(ngrp)], axis=1) * ps_ref[...]).astype(BF16)
         for b in range(nblk)], axis=0)

    x1 = (x_ref[...]
          + jnp.dot(a, wo_ref[0:POOL_WIDTH, :], preferred_element_type=F32)
          + jnp.dot(m_ref[...], wo_ref[POOL_WIDTH:, :], preferred_element_type=F32))
    x1_ref[...] = x1
    xn = x1 * lax.rsqrt(jnp.mean(x1 * x1, axis=-1, keepdims=True) + EPS) * g2_ref[...]
    _store_row_tiles(xn_ref, xn)

    lg = lax.dot_general(wr_ref[...], xn.astype(BF16), (((1,), (1,)), ((), ())),
                         preferred_element_type=F32) + br_ref[:, 0:1]
    gl = lg[0:N_GROUPS, :]
    gmax = jnp.max(gl, axis=0, keepdims=True)
    giota = lax.broadcasted_iota(I32, gl.shape, 0).astype(F32)
    g_idx = jnp.min(jnp.where(gl == gmax, giota, float(N_GROUPS)), axis=0, keepdims=True)
    g_w = 1.0 / jnp.sum(jnp.exp(gl - gmax), axis=0, keepdims=True)
    el = lg[8:8 + EPG, :]
    for gg in range(1, N_GROUPS):
        el = jnp.where(g_idx == float(gg), lg[8 + gg * EPG:8 + (gg + 1) * EPG, :], el)
    eiota = lax.broadcasted_iota(I32, el.shape, 0).astype(F32)
    top1 = jnp.max(el, axis=0, keepdims=True)
    i1 = jnp.min(jnp.where(el == top1, eiota, float(EPG)), axis=0, keepdims=True)
    el2 = jnp.where(eiota == i1, -jnp.inf, el)
    top2 = jnp.max(el2, axis=0, keepdims=True)
    i2 = jnp.min(jnp.where(el2 == top2, eiota, float(EPG)), axis=0, keepdims=True)
    ex = jnp.exp(top2 - top1)
    w1 = g_w / (1.0 + ex)
    w2 = g_w * ex / (1.0 + ex)
    e1 = g_idx * float(EPG) + i1
    e2 = g_idx * float(EPG) + i2
    eid_ref[0:1, :] = e1.astype(I32)
    eid_ref[1:2, :] = e2.astype(I32)

    xiota = lax.broadcasted_iota(I32, (N_EXPERTS, tm), 0).astype(F32)
    oh1 = xiota == e1
    oh2 = xiota == e2
    oh = jnp.where(oh1 | oh2, 1.0, 0.0)
    before = jnp.dot(oh.astype(BF16), su_ref[...], preferred_element_type=F32)
    base = run_sc[:, 0:1] + before
    rank_ref[0:1, :] = jnp.sum(jnp.where(oh1, base, 0.0), axis=0, keepdims=True).astype(I32)
    rank_ref[1:2, :] = jnp.sum(jnp.where(oh2, base, 0.0), axis=0, keepdims=True).astype(I32)
    run_new = run_sc[...] + jnp.sum(oh, axis=1, keepdims=True)
    run_sc[...] = run_new
    cnt_ref[...] = run_new

    riota = lax.broadcasted_iota(I32, (LANES, tm), 0)
    wt = jnp.where(riota == 0, w1, jnp.where(riota == 1, w2, 0.0))
    ew_ref[...] = wt.T


def _mix(x2, u, m, band_hi, band_lo, pool_w, pool_scale, w_out, norm2_g, wr, br, su, counts_in, seq_len):
    T = x2.shape[0]
    tm = TM_MIX
    assert seq_len % tm == 0 and seq_len >= 2 * POOL_BLOCK
    hb = tm // POOL_HALO
    nhb = T // POOL_HALO
    band_spec = pl.BlockSpec((3, len(POOL_WINDOWS), POOL_BLOCK, POOL_BLOCK + 2 * POOL_HALO), lambda i: (0, 0, 0, 0))
    tok = lambda i: (i, 0)
    fix = lambda i: (0, 0)
    outs = (
        jax.ShapeDtypeStruct((T, D_MODEL), F32),
        jax.ShapeDtypeStruct((T * ROW_SUB, LANES), F32),
        jax.ShapeDtypeStruct((2, T), I32),
        jax.ShapeDtypeStruct((2, T), I32),
        jax.ShapeDtypeStruct((T, LANES), F32),
        jax.ShapeDtypeStruct((N_EXPERTS, LANES), F32),
    )
    return pl.pallas_call(
        functools.partial(_mix_kernel, seq_len=seq_len),
        grid=(T // tm,),
        in_specs=[
            pl.BlockSpec((tm, D_MODEL), tok),
            pl.BlockSpec((tm, POOL_WIDTH), tok),
            pl.BlockSpec((POOL_HALO, POOL_WIDTH), lambda i: (jnp.maximum(i * hb - 1, 0), 0)),
            pl.BlockSpec((POOL_HALO, POOL_WIDTH), lambda i: (jnp.minimum((i + 1) * hb, nhb - 1), 0)),
            pl.BlockSpec((tm, MLSTM_WIDTH), tok),
            band_spec,
            band_spec,
            pl.BlockSpec((len(POOL_WINDOWS), POOL_GROUP, POOL_GROUP), lambda i: (0, 0, 0)),
            pl.BlockSpec((1, POOL_WIDTH), fix),
            pl.BlockSpec((D_MODEL, D_MODEL), fix),
            pl.BlockSpec((1, D_MODEL), fix),
            pl.BlockSpec((ROUTER_ROWS, D_MODEL), fix),
            pl.BlockSpec((ROUTER_ROWS, LANES), fix),
            pl.BlockSpec((tm, tm), fix),
            pl.BlockSpec((N_EXPERTS, LANES), fix),
        ],
        out_specs=[
            pl.BlockSpec((tm, D_MODEL), tok),
            pl.BlockSpec((tm * ROW_SUB, LANES), tok),
            pl.BlockSpec((2, tm), lambda i: (0, i)),
            pl.BlockSpec((2, tm), lambda i: (0, i)),
            pl.BlockSpec((tm, LANES), tok),
            pl.BlockSpec((N_EXPERTS, LANES), fix),
        ],
        out_shape=outs,
        scratch_shapes=[pltpu.VMEM((tm + 2 * POOL_HALO, POOL_WIDTH), BF16), pltpu.VMEM((N_EXPERTS, LANES), F32)],
        compiler_params=_cparams(1),
        name="mix_router",
    )(x2, u, u, u, m, band_hi, band_lo, pool_w, pool_scale, w_out, norm2_g, wr, br, su, counts_in)


def _row_copy(src, dst, sem):
    return pltpu.make_async_copy(src, dst, sem)


def _push_kernel(zs_ref, d0_ref, d1_ref, *refs, tiles):
    xn_refs = refs[:len(tiles)]
    rows_ref, zero_sc, sem, zsem = refs[len(tiles):]
    tm = TM_ROWS
    i = pl.program_id(0)

    @pl.when(i == 0)
    def _():
        zero_sc[...] = jnp.zeros_like(zero_sc)

        def fill(e):
            start = pl.multiple_of(jnp.maximum(zs_ref[e], 0) * ROW_SUB, ROW_SUB)
            return _row_copy(zero_sc, rows_ref.at[pl.ds(start, BM_EXPERT * ROW_SUB), :], zsem)

        for e in range(2 * N_EXPERTS):
            @pl.when(zs_ref[e] >= 0)
            def _():
                fill(e).start()
        for e in range(2 * N_EXPERTS):
            @pl.when(zs_ref[e] >= 0)
            def _():
                fill(e).wait()

    def push_tile(xn_ref):
        def copies(t):
            src = _row_tile(xn_ref, t)
            return (_row_copy(src, _row_tile(rows_ref, d0_ref[0, 0, t]), sem),
                    _row_copy(src, _row_tile(rows_ref, d1_ref[0, 0, t]), sem))

        def issue(t, carry):
            c0, c1 = copies(t)
            c0.start(priority=0)
            c1.start(priority=1)
            return carry

        def drain(t, carry):
            c0, c1 = copies(t)
            c0.wait()
            c1.wait()
            return carry

        lax.fori_loop(0, tm, issue, 0, unroll=4)
        lax.fori_loop(0, tm, drain, 0, unroll=4)

    first = 0
    for xn_ref, n in zip(xn_refs, tiles):
        @pl.when((i >= first) & (i < first + n))
        def _():
            push_tile(xn_ref)
        first += n


def _push(zero_start, dest0, dest1, xns, n_rows):
    tm = TM_ROWS
    tiles = tuple(xn.shape[0] // (ROW_SUB * tm) for xn in xns)
    smem_blk = pl.BlockSpec((1, 1, tm), lambda i, zs: (i, 0, 0), memory_space=pltpu.SMEM)
    in_specs = [smem_blk, smem_blk]
    first = 0
    for n in tiles:
        in_specs.append(pl.BlockSpec((tm * ROW_SUB, LANES),
                                     lambda i, zs, first=first, n=n: (jnp.clip(i - first, 0, n - 1), 0)))
        first += n
    grid_spec = pltpu.PrefetchScalarGridSpec(
        num_scalar_prefetch=1,
        grid=(sum(tiles),),
        in_specs=in_specs,
        out_specs=pl.BlockSpec(memory_space=pl.ANY),
        scratch_shapes=[pltpu.VMEM((BM_EXPERT * ROW_SUB, LANES), F32), pltpu.SemaphoreType.DMA(()),
                        pltpu.SemaphoreType.DMA(())],
    )
    return pl.pallas_call(
        functools.partial(_push_kernel, tiles=tiles),
        grid_spec=grid_spec,
        out_shape=jax.ShapeDtypeStruct((n_rows * ROW_SUB, LANES), F32),
        compiler_params=_cparams(1),
        name="row_push",
    )(zero_start, dest0, dest1, *xns)


def _expert_kernel(be_ref, nv_ref, x_ref, wg_ref, wu_ref, wd_ref, y_ref, wg_sc, wu_sc, wd_sc):
    i = pl.program_id(0)
    valid = i < nv_ref[0]
    new_expert = (i == 0) | (be_ref[i] != be_ref[jnp.maximum(i - 1, 0)])

    @pl.when(valid & new_expert)
    def _():
        wg_sc[...] = wg_ref[0].astype(BF16)
        wu_sc[...] = wu_ref[0].astype(BF16)
        wd_sc[...] = wd_ref[0].astype(BF16)

    @pl.when(valid)
    def _():
        xb = _load_row_tiles(x_ref).astype(BF16)
        gate = jnp.dot(xb, wg_sc[...], preferred_element_type=F32)
        up = jnp.dot(xb, wu_sc[...], preferred_element_type=F32)
        hb = (gate * jax.nn.sigmoid(gate) * up).astype(BF16)
        _store_row_tiles(y_ref, jnp.dot(hb, wd_sc[...], preferred_element_type=F32))

    @pl.when(jnp.logical_not(valid))
    def _():
        y_ref[...] = jnp.zeros_like(y_ref)


def _experts(block_e, n_valid, rows, w_gate, w_up, w_down):
    P = rows.shape[0] // ROW_SUB
    bm = BM_EXPERT
    last = lambda i, nv: jnp.minimum(i, jnp.maximum(nv[0] - 1, 0))
    row_blk = lambda i, be, nv: (last(i, nv), 0)
    w_blk = lambda i, be, nv: (be[last(i, nv)], 0, 0)
    grid_spec = pltpu.PrefetchScalarGridSpec(
        num_scalar_prefetch=2,
        grid=(P // bm,),
        in_specs=[
            pl.BlockSpec((bm * ROW_SUB, LANES), row_blk),
            pl.BlockSpec((1, D_MODEL, D_EXPERT), w_blk),
            pl.BlockSpec((1, D_MODEL, D_EXPERT), w_blk),
            pl.BlockSpec((1, D_EXPERT, D_MODEL), w_blk),
        ],
        out_specs=pl.BlockSpec((bm * ROW_SUB, LANES), lambda i, be, nv: (i, 0)),
        scratch_shapes=[pltpu.VMEM((D_MODEL, D_EXPERT), BF16), pltpu.VMEM((D_MODEL, D_EXPERT), BF16),
                        pltpu.VMEM((D_EXPERT, D_MODEL), BF16)],
    )
    return pl.pallas_call(
        _expert_kernel,
        grid_spec=grid_spec,
        out_shape=jax.ShapeDtypeStruct((P * ROW_SUB, LANES), F32),
        compiler_params=_cparams(1),
        name="experts",
    )(block_e, n_valid, rows, w_gate, w_up, w_down)


def _combine_kernel(d0_ref, d1_ref, d0n_ref, d1n_ref, x1_ref, ew_ref, gf_ref, y_ref, out_ref, buf, sem, *, n_tiles):
    tm = TM_ROWS
    i = pl.program_id(0)
    slot = i % 2

    def copies(d0, d1, sl, t):
        return (_row_copy(_row_tile(y_ref, d0[0, 0, t]), _row_tile(buf.at[sl, 0], t), sem.at[sl]),
                _row_copy(_row_tile(y_ref, d1[0, 0, t]), _row_tile(buf.at[sl, 1], t), sem.at[sl]))

    def gather_start(d0, d1, sl):
        def body(t, carry):
            c0, c1 = copies(d0, d1, sl, t)
            c0.start(priority=0)
            c1.start(priority=1)
            return carry
        lax.fori_loop(0, tm, body, 0, unroll=4)

    @pl.when(i == 0)
    def _():
        gather_start(d0_ref, d1_ref, 0)

    @pl.when(i + 1 < n_tiles)
    def _():
        gather_start(d0n_ref, d1n_ref, 1 - slot)

    def drain(t, carry):
        c0, c1 = copies(d0_ref, d1_ref, slot, t)
        c0.wait()
        c1.wait()
        return carry

    lax.fori_loop(0, tm, drain, 0, unroll=4)

    ew = ew_ref[...]
    x = (x1_ref[...] + _load_row_tiles(buf.at[slot, 0]) * ew[:, 0:1]
         + _load_row_tiles(buf.at[slot, 1]) * ew[:, 1:2])
    out_ref[...] = x * lax.rsqrt(jnp.mean(x * x, axis=-1, keepdims=True) + EPS) * gf_ref[...]


def _combine(dest0, dest1, x1, ew, final_g, yrows):
    T = x1.shape[0]
    tm = TM_ROWS
    nt = T // tm
    smem_blk = pl.BlockSpec((1, 1, tm), lambda i: (i, 0, 0), memory_space=pltpu.SMEM)
    smem_next = pl.BlockSpec((1, 1, tm), lambda i: (jnp.minimum(i + 1, nt - 1), 0, 0), memory_space=pltpu.SMEM)
    tok = lambda i: (i, 0)
    return pl.pallas_call(
        functools.partial(_combine_kernel, n_tiles=nt),
        grid=(nt,),
        in_specs=[smem_blk, smem_blk, smem_next, smem_next,
                  pl.BlockSpec((tm, D_MODEL), tok),
                  pl.BlockSpec((tm, LANES), tok),
                  pl.BlockSpec((1, D_MODEL), lambda i: (0, 0)),
                  pl.BlockSpec(memory_space=pl.ANY)],
        out_specs=pl.BlockSpec((tm, D_MODEL), tok),
        out_shape=jax.ShapeDtypeStruct((T, D_MODEL), F32),
        scratch_shapes=[pltpu.VMEM((2, 2, tm * ROW_SUB, LANES), F32), pltpu.SemaphoreType.DMA((2,))],
        compiler_params=_cparams(1),
        name="combine",
    )(dest0, dest1, dest0, dest1, x1, ew, final_g, yrows)


def _pool_bands():
    t = jnp.arange(POOL_BLOCK)[:, None]
    e = jnp.arange(POOL_BLOCK + 2 * POOL_HALO)[None, :]
    bands = []
    for kind in range(3):
        per_w = []
        for w in POOL_WINDOWS:
            half = w // 2
            win = (e >= t + POOL_HALO - half) & (e < t + POOL_HALO + half)
            lo_edge = jnp.maximum(t - half, 0) if kind == 0 else t - half
            hi_edge = jnp.minimum(t + half, POOL_BLOCK) if kind == 2 else t + half
            per_w.append(jnp.where(win, 1.0 / (hi_edge - lo_edge).astype(F32), 0.0))
        bands.append(jnp.stack(per_w))
    band = jnp.stack(bands)
    hi = band.astype(BF16)
    return hi, (band - hi.astype(F32)).astype(BF16)


def _prepare_params(norm1_g, w_in, pool_w, pool_scale, gate_b, head_norm_g, w_out, norm2_g,
                    router_group_w, router_group_b, router_expert_w, router_expert_b,
                    expert_w_gate, expert_w_up, expert_w_down, final_norm_g):
    w = w_in[0]
    n_main = 5 * 512
    w_main = jnp.concatenate([w[:, 0:512], w[:, 1024:1536]], axis=1).astype(BF16)
    w_feat_t = jnp.concatenate([w[:, 512:1024], w[:, 1536:n_main]], axis=1).T.astype(BF16)
    w_gate = jnp.pad(w[:, n_main:], ((0, 0), (0, LANES - N_GATES))).astype(BF16)
    gb = jnp.pad(gate_b[0], (0, LANES - N_GATES)).reshape(1, LANES)
    wr = jnp.zeros((ROUTER_ROWS, D_MODEL), F32)
    wr = wr.at[0:N_GROUPS].set(router_group_w[0].T)
    wr = wr.at[8:].set(router_expert_w[0].reshape(D_MODEL, N_EXPERTS).T)
    br = jnp.zeros((ROUTER_ROWS,), F32)
    br = br.at[0:N_GROUPS].set(router_group_b[0])
    br = br.at[8:].set(router_expert_b[0].reshape(N_EXPERTS))
    br = jnp.broadcast_to(br[:, None], (ROUTER_ROWS, LANES))
    idx = jnp.arange(CHUNK)
    tri_f = (idx[None, :] <= idx[:, None]).astype(F32)
    tri_b = (idx[None, :] >= idx[:, None]).astype(F32)
    tidx = jnp.arange(TM_MIX)
    su = (tidx[:, None] < tidx[None, :]).astype(BF16)
    band_hi, band_lo = _pool_bands()
    return dict(
        norm1_g=norm1_g[0].reshape(1, D_MODEL), w_main=w_main, w_feat_t=w_feat_t, w_gate=w_gate, gate_b=gb,
        pool_w=pool_w[0].astype(BF16), pool_scale=pool_scale[0].reshape(1, POOL_WIDTH),
        head_g_t=jnp.broadcast_to(head_norm_g[0][:, None], (MLSTM_WIDTH, LANES)), w_out=w_out[0].astype(BF16),
        norm2_g=norm2_g[0].reshape(1, D_MODEL), wr=wr.astype(BF16), br=br, tri_f=tri_f, tri_b=tri_b, su=su, band_hi=band_hi, band_lo=band_lo,
        e_gate=expert_w_gate[0], e_up=expert_w_up[0], e_down=expert_w_down[0],
        final_g=final_norm_g.reshape(1, D_MODEL),
    )


def _mixer(x, p, counts_in):
    B, S, _ = x.shape
    T = B * S
    x2 = x.reshape(T, D_MODEL)
    u, k, qt, vt, ot, gt = _inproj(x2, p["norm1_g"], p["w_main"], p["w_feat_t"], p["w_gate"], p["gate_b"])
    hft = _mlstm(k, qt, vt, gt, p["tri_f"], p["tri_b"], B, S, rev=False)
    m = _mlstm(k, qt, vt, gt, p["tri_b"], p["tri_f"], B, S, rev=True, hft=hft, ot=ot, head_g_t=p["head_g_t"])
    return _mix(x2, u, m, p["band_hi"], p["band_lo"], p["pool_w"], p["pool_scale"], p["w_out"], p["norm2_g"],
                p["wr"], p["br"], p["su"], counts_in, S)


def _encoders(xs, p):
    parts = []
    cnt = jnp.zeros((N_EXPERTS, LANES), F32)
    for x in xs:
        x1, xn, eid, rank, ew, cnt = _mixer(x, p, cnt)
        parts.append((x1, xn, eid, rank, ew))

    bm = BM_EXPERT
    n_tok = sum(part[0].shape[0] for part in parts)
    nb = (2 * n_tok) // bm + N_EXPERTS
    counts = cnt[:, 0].astype(I32)
    pcounts = ((counts + bm - 1) // bm) * bm
    pends = jnp.cumsum(pcounts)
    pstarts = pends - pcounts
    block_e = jnp.minimum(jnp.sum(pends[None, :] <= (jnp.arange(nb, dtype=I32) * bm)[:, None], axis=1),
                          N_EXPERTS - 1).astype(I32)
    n_valid = (pends[-1:] // bm).astype(I32)
    last_block = jnp.where(counts > 0, pends - bm, -1)
    tail_start = pends[-1] + jnp.arange(N_EXPERTS, dtype=I32) * bm
    tail_block = jnp.where(tail_start < nb * bm, tail_start, -1)
    zero_start = jnp.concatenate([last_block, tail_block]).astype(I32)
    e_ids = jnp.arange(N_EXPERTS, dtype=I32)[:, None, None]

    dests = []
    for x1, xn, eid, rank, ew in parts:
        nt = x1.shape[0] // TM_ROWS
        dest = rank + jnp.sum(jnp.where(eid[None] == e_ids, pstarts[:, None, None], 0), axis=0)
        dests.append((dest[0].reshape(nt, 1, TM_ROWS), dest[1].reshape(nt, 1, TM_ROWS)))
    rows = _push(zero_start, jnp.concatenate([d[0] for d in dests]), jnp.concatenate([d[1] for d in dests]),
                 [part[1] for part in parts], nb * bm)
    yrows = _experts(block_e, n_valid, rows, p["e_gate"], p["e_up"], p["e_down"])
    return tuple(_combine(d0, d1, x1, ew, p["final_g"], yrows).reshape(x.shape)
                 for x, (d0, d1), (x1, _, _, _, ew) in zip(xs, dests, parts))


def kernel(x_prompt, x_sample, norm1_g, w_in, pool_w, pool_scale, gate_b, head_norm_g, w_out, norm2_g,
           router_group_w, router_group_b, router_expert_w, router_expert_b,
           expert_w_gate, expert_w_up, expert_w_down, final_norm_g):
    p = _prepare_params(norm1_g, w_in, pool_w, pool_scale, gate_b, head_norm_g, w_out, norm2_g,
                        router_group_w, router_group_b, router_expert_w, router_expert_b,
                        expert_w_gate, expert_w_up, expert_w_down, final_norm_g)
    return _encoders((x_prompt, x_sample), p)
```

```python
import functools

import jax
import jax.numpy as jnp
from jax import lax
from jax.experimental import pallas as pl
from jax.experimental.pallas import tpu as pltpu

F32 = jnp.float32
BF16 = jnp.bfloat16
I32 = jnp.int32

D_MODEL = 1024
POOL_WIDTH = 512
POOL_WINDOWS = (2, 4, 8, 16)
POOL_GROUP = 128
POOL_HALO = 16
POOL_BLOCK = 128
MLSTM_WIDTH = 512
HEADS = 4
HEAD_DIM = 128
CHUNK = 128
N_GATES = 16
N_GROUPS = 4
EPG = 8
N_EXPERTS = 32
D_EXPERT = 512
EPS = 1e-6
LANES = 128
ROUTER_ROWS = 8 + N_EXPERTS

MLSTM_CHUNKS_PER_STEP = 16
STATE_ROWS = HEAD_DIM + 16
TM_PROJ = 1024
TM_MIX = 1024
TM_ROWS = 256
ROW_SUB = D_MODEL // LANES
BM_EXPERT = 512
VMEM_LIMIT = 56 * 1024 * 1024

HIGHEST = lax.Precision.HIGHEST


def _cparams(n_axes):
    return pltpu.CompilerParams(dimension_semantics=("arbitrary",) * n_axes,
                                vmem_limit_bytes=VMEM_LIMIT)


def _store_row_tiles(ref, val):
    rows = val.shape[0]
    for j in range(ROW_SUB):
        ref[pl.ds(j, rows, stride=ROW_SUB), :] = val[:, j * LANES:(j + 1) * LANES]


def _load_row_tiles(ref):
    rows = ref.shape[0] // ROW_SUB
    return jnp.concatenate([ref[pl.ds(j, rows, stride=ROW_SUB), :] for j in range(ROW_SUB)], axis=1)


def _row_tile(ref, r):
    return ref.at[pl.ds(pl.multiple_of(r * ROW_SUB, ROW_SUB), ROW_SUB), :]


def _inproj_kernel(x_ref, g_ref, w_ref, wt_ref, wg_ref, gb_ref, u_ref, k_ref, qt_ref, vt_ref, ot_ref, gt_ref):
    x = x_ref[...]
    inv = lax.rsqrt(jnp.mean(x * x, axis=-1, keepdims=True) + EPS)
    xn = (x * inv * g_ref[...]).astype(BF16)

    def sec(i):
        return jnp.dot(xn, w_ref[:, i * 512:(i + 1) * 512], preferred_element_type=F32)

    def sec_t(i):
        return lax.dot_general(wt_ref[i * 512:(i + 1) * 512, :], xn, (((1,), (1,)), ((), ())),
                               preferred_element_type=F32)

    u_ref[...] = sec(0)
    k_ref[...] = (sec(1) * (HEAD_DIM ** -0.5)).astype(BF16)
    qt_ref[...] = sec_t(0).astype(BF16)
    vt_ref[...] = sec_t(1).astype(BF16)
    ot_ref[...] = sec_t(2)
    g = jnp.dot(xn, wg_ref[...], preferred_element_type=F32) + gb_ref[...]
    lane = lax.broadcasted_iota(I32, g.shape, 1)
    logsig = jnp.minimum(g, 0.0) - jnp.log1p(jnp.exp(-jnp.abs(g)))
    gt_ref[...] = jnp.where(lane >= 2 * HEADS, logsig, g)


def _inproj(x2, norm_g, w_main, w_feat_t, w_gate, gate_b):
    T = x2.shape[0]
    tm = TM_PROJ
    tok = lambda i: (i, 0)
    tok_t = lambda i: (0, i)
    fix = lambda i: (0, 0)
    outs = (
        jax.ShapeDtypeStruct((T, POOL_WIDTH), F32),
        jax.ShapeDtypeStruct((T, MLSTM_WIDTH), BF16),
        jax.ShapeDtypeStruct((MLSTM_WIDTH, T), BF16),
        jax.ShapeDtypeStruct((MLSTM_WIDTH, T), BF16),
        jax.ShapeDtypeStruct((MLSTM_WIDTH, T), F32),
        jax.ShapeDtypeStruct((T, LANES), F32),
    )
    return pl.pallas_call(
        _inproj_kernel,
        grid=(T // tm,),
        in_specs=[
            pl.BlockSpec((tm, D_MODEL), tok),
            pl.BlockSpec((1, D_MODEL), fix),
            pl.BlockSpec((D_MODEL, 2 * 512), fix),
            pl.BlockSpec((3 * 512, D_MODEL), fix),
            pl.BlockSpec((D_MODEL, LANES), fix),
            pl.BlockSpec((1, LANES), fix),
        ],
        out_specs=[
            pl.BlockSpec((tm, 512), tok), pl.BlockSpec((tm, 512), tok), pl.BlockSpec((512, tm), tok_t),
            pl.BlockSpec((512, tm), tok_t), pl.BlockSpec((512, tm), tok_t), pl.BlockSpec((tm, LANES), tok),
        ],
        out_shape=outs,
        compiler_params=_cparams(1),
        name="inproj",
    )(x2, norm_g, w_main, w_feat_t, w_gate, gate_b)


def _mlstm_kernel(*refs, rev):
    if rev:
        (k_ref, qt_ref, vt_ref, gt_ref, tri_ref, trit_ref, hft_ref, ot_ref, hgt_ref, out_ref,
         c_sc, m_sc, a_sc, upd_sc) = refs
    else:
        (k_ref, qt_ref, vt_ref, gt_ref, tri_ref, trit_ref, out_ref, c_sc, m_sc, a_sc, upd_sc) = refs
    L = CHUNK

    @pl.when(pl.program_id(1) == 0)
    def _():
        c_sc[...] = jnp.zeros_like(c_sc)
        m_sc[...] = jnp.zeros_like(m_sc)

    tri = tri_ref[...]
    mask_t = trit_ref[...] > 0.0
    sub8 = lax.broadcasted_iota(I32, (8, L), 0)
    sub_aug = lax.broadcasted_iota(I32, (STATE_ROWS - HEAD_DIM, L), 0)

    nch = MLSTM_CHUNKS_PER_STEP
    order = [nch - 1 - i if rev else i for i in range(nch)]
    hd = range(HEADS)
    cs = [HEADS * rev + h for h in hd]
    hsl = [slice(h * HEAD_DIM, (h + 1) * HEAD_DIM) for h in hd]
    rows = [slice(ci * L, (ci + 1) * L) for ci in range(nch)]
    pairs = [(ci, h) for ci in order for h in hd]
    row = lambda t, c: t[c:c + 1, :]
    last = 0 if rev else L - 1

    g = {ci: gt_ref[rows[ci], :] for ci in order}
    b_all = {ci: jnp.dot(tri, g[ci], precision=HIGHEST, preferred_element_type=F32) for ci in order}
    ba = {ci: pltpu.roll(b_all[ci], LANES - 2 * HEADS, axis=1) for ci in order}
    r_col = {ci: g[ci] - ba[ci] for ci in order}
    i_r = {ci: g[ci].T[0:8, :] for ci in order}
    b_r = {ci: ba[ci].T[0:8, :] for ci in order}
    b_last = {ci: b_r[ci][:, last:last + 1] for ci in order}
    logu = {ci: b_last[ci] - b_r[ci] + i_r[ci] for ci in order}
    mu = {ci: jnp.max(logu[ci], axis=1, keepdims=True) for ci in order}
    u_r = {ci: jnp.exp(logu[ci] - mu[ci]) for ci in order}

    kq = {p: jnp.dot(k_ref[rows[p[0]], hsl[p[1]]], qt_ref[hsl[p[1]], rows[p[0]]],
                     preferred_element_type=F32) for p in pairs}
    for p in pairs:
        ci, h = p
        u_row = row(u_r[ci], cs[h])
        vtu = vt_ref[hsl[h], rows[ci]].astype(F32) * u_row
        aug = jnp.concatenate([vtu, jnp.where(sub_aug == 0, u_row, 0.0)], axis=0).astype(BF16)
        upd_sc[ci * HEADS + h] = jnp.dot(aug, k_ref[rows[ci], hsl[h]], preferred_element_type=F32)
    logw = {p: jnp.where(mask_t, row(b_r[p[0]], cs[p[1]]) + r_col[p[0]][:, cs[p[1]]:cs[p[1]] + 1], -jnp.inf)
            for p in pairs}
    mw = {p: jnp.max(logw[p], axis=0, keepdims=True) for p in pairs}
    st = {p: kq[p] * jnp.exp(logw[p] - mw[p]) for p in pairs}
    dl = {p: jnp.sum(st[p], axis=0, keepdims=True) for p in pairs}
    for p in pairs:
        ci, h = p
        a_sc[ci * HEADS + h] = jnp.dot(vt_ref[hsl[h], rows[ci]], st[p].astype(BF16),
                                       preferred_element_type=F32)
    mw_r, dl_r = {}, {}
    for ci in order:
        t_mw = jnp.zeros((8, L), F32)
        t_dl = jnp.zeros((8, L), F32)
        for h in hd:
            t_mw = jnp.where(sub8 == cs[h], mw[(ci, h)], t_mw)
            t_dl = jnp.where(sub8 == cs[h], dl[(ci, h)], t_dl)
        mw_r[ci], dl_r[ci] = t_mw, t_dl

    for ci in order:
        m_prev = m_sc[...]
        m_new = jnp.maximum(b_last[ci] + m_prev, mu[ci])
        decay = jnp.exp(b_last[ci] + m_prev - m_new)
        u_scale = jnp.exp(mu[ci] - m_new)
        m_sc[...] = m_new
        inter_log = b_r[ci] + m_prev
        m_row = jnp.maximum(inter_log, mw_r[ci])
        w_intra = jnp.exp(mw_r[ci] - m_row)
        w_cross = jnp.exp(inter_log - m_row)
        c_old = [c_sc[h] for h in hd]
        x = [jnp.dot(c_old[h].astype(BF16), qt_ref[hsl[h], rows[ci]], preferred_element_type=F32)
             for h in hd]
        for h in hd:
            c_sc[h] = (decay[cs[h]:cs[h] + 1, 0:1] * c_old[h]
                       + u_scale[cs[h]:cs[h] + 1, 0:1] * upd_sc[ci * HEADS + h])
        dc_r = jnp.zeros((8, L), F32)
        for h in hd:
            dc_r = jnp.where(sub8 == cs[h], x[h][HEAD_DIM:HEAD_DIM + 1, :], dc_r)
        den = w_intra * dl_r[ci] + w_cross * dc_r
        rden = 1.0 / jnp.maximum(jnp.abs(den), jnp.exp(-m_row))
        f_intra = w_intra * rden
        f_cross = w_cross * rden
        ht = [row(f_intra, cs[h]) * a_sc[ci * HEADS + h] + row(f_cross, cs[h]) * x[h][0:HEAD_DIM, :] for h in hd]
        if rev:
            hsum = [hft_ref[hsl[h], rows[ci]] + ht[h] for h in hd]
            inv = [lax.rsqrt(jnp.mean(hsum[h] * hsum[h], axis=0, keepdims=True) + EPS) for h in hd]
            for h in hd:
                mt = jax.nn.sigmoid(ot_ref[hsl[h], rows[ci]]) * (hsum[h] * inv[h] * hgt_ref[hsl[h], :])
                out_ref[rows[ci], hsl[h]] = mt.T.astype(out_ref.dtype)
        else:
            for h in hd:
                out_ref[hsl[h], rows[ci]] = ht[h]


def _mlstm(k, qt, vt, gt, tri, tri_t, B, S, rev, hft=None, ot=None, head_g_t=None):
    T = B * S
    nrow = CHUNK * MLSTM_CHUNKS_PER_STEP
    nc = S // nrow
    if rev:
        step = lambda b, c: b * nc + (nc - 1 - c)
    else:
        step = lambda b, c: b * nc + c
    tok = lambda b, c: (step(b, c), 0)
    tok_t = lambda b, c: (0, step(b, c))
    fix = lambda b, c: (0, 0)
    blk = lambda w: pl.BlockSpec((nrow, w), tok)
    blk_t = pl.BlockSpec((MLSTM_WIDTH, nrow), tok_t)
    in_specs = [blk(512), blk_t, blk_t, blk(LANES), pl.BlockSpec((CHUNK, CHUNK), fix),
                pl.BlockSpec((CHUNK, CHUNK), fix)]
    args = [k, qt, vt, gt, tri, tri_t]
    if rev:
        in_specs += [blk_t, blk_t, pl.BlockSpec((MLSTM_WIDTH, LANES), fix)]
        args += [hft, ot, head_g_t]
        out_spec, out_shape = blk(512), jax.ShapeDtypeStruct((T, MLSTM_WIDTH), BF16)
    else:
        out_spec, out_shape = blk_t, jax.ShapeDtypeStruct((MLSTM_WIDTH, T), F32)
    return pl.pallas_call(
        functools.partial(_mlstm_kernel, rev=rev),
        grid=(B, nc),
        in_specs=in_specs,
        out_specs=out_spec,
        out_shape=out_shape,
        scratch_shapes=[pltpu.VMEM((HEADS, STATE_ROWS, HEAD_DIM), F32), pltpu.VMEM((8, LANES), F32),
                        pltpu.VMEM((MLSTM_CHUNKS_PER_STEP * HEADS, HEAD_DIM, CHUNK), F32),
                        pltpu.VMEM((MLSTM_CHUNKS_PER_STEP * HEADS, STATE_ROWS, HEAD_DIM), F32)],
        compiler_params=_cparams(2),
        name="mlstm_bwd" if rev else "mlstm_fwd",
    )(*args)


def _mix_kernel(x_ref, u_ref, up_ref, un_ref, m_ref, bh_ref, bl_ref, pw_ref, ps_ref, wo_ref, g2_ref, wr_ref, br_ref,
                su_ref, cin_ref, x1_ref, xn_ref, eid_ref, rank_ref, ew_ref, cnt_ref, ext_sc, run_sc, *, seq_len):
    tm = TM_MIX
    i = pl.program_id(0)
    tps = seq_len // tm
    it = i % tps

    @pl.when(i == 0)
    def _():
        run_sc[...] = cin_ref[...]

    ext_sc[0:POOL_HALO, :] = jnp.where(it == 0, 0.0, up_ref[...]).astype(BF16)
    ext_sc[POOL_HALO:POOL_HALO + tm, :] = u_ref[...].astype(BF16)
    ext_sc[POOL_HALO + tm:, :] = jnp.where(it == tps - 1, 0.0, un_ref[...]).astype(BF16)
    nblk = tm // POOL_BLOCK
    ngrp = len(POOL_WINDOWS)
    cells = [(b, gi) for b in range(nblk) for gi in range(ngrp)]
    rs = [slice(b * POOL_BLOCK, (b + 1) * POOL_BLOCK) for b in range(nblk)]
    ls = [slice(gi * POOL_GROUP, (gi + 1) * POOL_GROUP) for gi in range(ngrp)]
    first = jnp.where(it == 0, 0, 1)
    final = jnp.where(it == tps - 1, 2, 1)

    def mean(b, gi):
        slab = ext_sc[b * POOL_BLOCK:(b + 1) * POOL_BLOCK + 2 * POOL_HALO, ls[gi]]
        if 0 < b < nblk - 1:
            return jnp.dot(bh_ref[1, gi], slab, preferred_element_type=F32)
        kind = first if b == 0 else final
        return (jnp.dot(bh_ref[kind, gi], slab, preferred_element_type=F32)
                + jnp.dot(bl_ref[kind, gi], slab, preferred_element_type=F32))

    means = {c: mean(*c) for c in cells}
    dev = {c: (means[c] - u_ref[rs[c[0]], ls[c[1]]]).astype(BF16) for c in cells}
    lin = {c: jnp.dot(dev[c], pw_ref[c[1]], preferred_element_type=F32) for c in cells}
    a = jnp.concatenate(
        [(jnp.concatenate([lin[(b, gi)] for gi in range(ngrp)], axis=1) * ps_ref[...]).astype(BF16)
         for b in range(nblk)], axis=0)

    x1 = (x_ref[...]
          + jnp.dot(a, wo_ref[0:POOL_WIDTH, :], preferred_element_type=F32)
          + jnp.dot(m_ref[...], wo_ref[POOL_WIDTH:, :], preferred_element_type=F32))
    x1_ref[...] = x1
    xn = x1 * lax.rsqrt(jnp.mean(x1 * x1, axis=-1, keepdims=True) + EPS) * g2_ref[...]
    _store_row_tiles(xn_ref, xn)

    lg = lax.dot_general(wr_ref[...], xn.astype(BF16), (((1,), (1,)), ((), ())),
                         preferred_element_type=F32) + br_ref[:, 0:1]
    gl = lg[0:N_GROUPS, :]
    gmax = jnp.max(gl, axis=0, keepdims=True)
    giota = lax.broadcasted_iota(I32, gl.shape, 0).astype(F32)
    g_idx = jnp.min(jnp.where(gl == gmax, giota, float(N_GROUPS)), axis=0, keepdims=True)
    g_w = 1.0 / jnp.sum(jnp.exp(gl - gmax), axis=0, keepdims=True)
    el = lg[8:8 + EPG, :]
    for gg in range(1, N_GROUPS):
        el = jnp.where(g_idx == float(gg), lg[8 + gg * EPG:8 + (gg + 1) * EPG, :], el)
    eiota = lax.broadcasted_iota(I32, el.shape, 0).astype(F32)
    top1 = jnp.max(el, axis=0, keepdims=True)
    i1 = jnp.min(jnp.where(el == top1, eiota, float(EPG)), axis=0, keepdims=True)
    el2 = jnp.where(eiota == i1, -jnp.inf, el)
    top2 = jnp.max(el2, axis=0, keepdims=True)
    i2 = jnp.min(jnp.where(el2 == top2, eiota, float(EPG)), axis=0, keepdims=True)
    ex = jnp.exp(top2 - top1)
    w1 = g_w / (1.0 + ex)
    w2 = g_w * ex / (1.0 + ex)
    e1 = g_idx * float(EPG) + i1
    e2 = g_idx * float(EPG) + i2
    eid_ref[0:1, :] = e1.astype(I32)
    eid_ref[1:2, :] = e2.astype(I32)

    xiota = lax.broadcasted_iota(I32, (N_EXPERTS, tm), 0).astype(F32)
    oh1 = xiota == e1
    oh2 = xiota == e2
    oh = jnp.where(oh1 | oh2, 1.0, 0.0)
    before = jnp.dot(oh.astype(BF16), su_ref[...], preferred_element_type=F32)
    base = run_sc[:, 0:1] + before
    rank_ref[0:1, :] = jnp.sum(jnp.where(oh1, base, 0.0), axis=0, keepdims=True).astype(I32)
    rank_ref[1:2, :] = jnp.sum(jnp.where(oh2, base, 0.0), axis=0, keepdims=True).astype(I32)
    run_new = run_sc[...] + jnp.sum(oh, axis=1, keepdims=True)
    run_sc[...] = run_new
    cnt_ref[...] = run_new

    riota = lax.broadcasted_iota(I32, (LANES, tm), 0)
    wt = jnp.where(riota == 0, w1, jnp.where(riota == 1, w2, 0.0))
    ew_ref[...] = wt.T


def _mix(x2, u, m, band_hi, band_lo, pool_w, pool_scale, w_out, norm2_g, wr, br, su, counts_in, seq_len):
    T = x2.shape[0]
    tm = TM_MIX
    assert seq_len % tm == 0 and seq_len >= 2 * POOL_BLOCK
    hb = tm // POOL_HALO
    nhb = T // POOL_HALO
    band_spec = pl.BlockSpec((3, len(POOL_WINDOWS), POOL_BLOCK, POOL_BLOCK + 2 * POOL_HALO), lambda i: (0, 0, 0, 0))
    tok = lambda i: (i, 0)
    fix = lambda i: (0, 0)
    outs = (
        jax.ShapeDtypeStruct((T, D_MODEL), F32),
        jax.ShapeDtypeStruct((T * ROW_SUB, LANES), F32),
        jax.ShapeDtypeStruct((2, T), I32),
        jax.ShapeDtypeStruct((2, T), I32),
        jax.ShapeDtypeStruct((T, LANES), F32),
        jax.ShapeDtypeStruct((N_EXPERTS, LANES), F32),
    )
    return pl.pallas_call(
        functools.partial(_mix_kernel, seq_len=seq_len),
        grid=(T // tm,),
        in_specs=[
            pl.BlockSpec((tm, D_MODEL), tok),
            pl.BlockSpec((tm, POOL_WIDTH), tok),
            pl.BlockSpec((POOL_HALO, POOL_WIDTH), lambda i: (jnp.maximum(i * hb - 1, 0), 0)),
            pl.BlockSpec((POOL_HALO, POOL_WIDTH), lambda i: (jnp.minimum((i + 1) * hb, nhb - 1), 0)),
            pl.BlockSpec((tm, MLSTM_WIDTH), tok),
            band_spec,
            band_spec,
            pl.BlockSpec((len(POOL_WINDOWS), POOL_GROUP, POOL_GROUP), lambda i: (0, 0, 0)),
            pl.BlockSpec((1, POOL_WIDTH), fix),
            pl.BlockSpec((D_MODEL, D_MODEL), fix),
            pl.BlockSpec((1, D_MODEL), fix),
            pl.BlockSpec((ROUTER_ROWS, D_MODEL), fix),
            pl.BlockSpec((ROUTER_ROWS, LANES), fix),
            pl.BlockSpec((tm, tm), fix),
            pl.BlockSpec((N_EXPERTS, LANES), fix),
        ],
        out_specs=[
            pl.BlockSpec((tm, D_MODEL), tok),
            pl.BlockSpec((tm * ROW_SUB, LANES), tok),
            pl.BlockSpec((2, tm), lambda i: (0, i)),
            pl.BlockSpec((2, tm), lambda i: (0, i)),
            pl.BlockSpec((tm, LANES), tok),
            pl.BlockSpec((N_EXPERTS, LANES), fix),
        ],
        out_shape=outs,
        scratch_shapes=[pltpu.VMEM((tm + 2 * POOL_HALO, POOL_WIDTH), BF16), pltpu.VMEM((N_EXPERTS, LANES), F32)],
        compiler_params=_cparams(1),
        name="mix_router",
    )(x2, u, u, u, m, band_hi, band_lo, pool_w, pool_scale, w_out, norm2_g, wr, br, su, counts_in)


def _row_copy(src, dst, sem):
    return pltpu.make_async_copy(src, dst, sem)


def _push_kernel(zs_ref, d0_ref, d1_ref, *refs, tiles):
    xn_refs = refs[:len(tiles)]
    rows_ref, zero_sc, sem, zsem = refs[len(tiles):]
    tm = TM_ROWS
    i = pl.program_id(0)

    @pl.when(i == 0)
    def _():
        zero_sc[...] = jnp.zeros_like(zero_sc)

        def fill(e):
            start = pl.multiple_of(jnp.maximum(zs_ref[e], 0) * ROW_SUB, ROW_SUB)
            return _row_copy(zero_sc, rows_ref.at[pl.ds(start, BM_EXPERT * ROW_SUB), :], zsem)

        for e in range(2 * N_EXPERTS):
            @pl.when(zs_ref[e] >= 0)
            def _():
                fill(e).start()
        for e in range(2 * N_EXPERTS):
            @pl.when(zs_ref[e] >= 0)
            def _():
                fill(e).wait()

    def push_tile(xn_ref):
        def copies(t):
            src = _row_tile(xn_ref, t)
            return (_row_copy(src, _row_tile(rows_ref, d0_ref[0, 0, t]), sem),
                    _row_copy(src, _row_tile(rows_ref, d1_ref[0, 0, t]), sem))

        def issue(t, carry):
            c0, c1 = copies(t)
            c0.start(priority=0)
            c1.start(priority=1)
            return carry

        def drain(t, carry):
            c0, c1 = copies(t)
            c0.wait()
            c1.wait()
            return carry

        lax.fori_loop(0, tm, issue, 0, unroll=4)
        lax.fori_loop(0, tm, drain, 0, unroll=4)

    first = 0
    for xn_ref, n in zip(xn_refs, tiles):
        @pl.when((i >= first) & (i < first + n))
        def _():
            push_tile(xn_ref)
        first += n


def _push(zero_start, dest0, dest1, xns, n_rows):
    tm = TM_ROWS
    tiles = tuple(xn.shape[0] // (ROW_SUB * tm) for xn in xns)
    smem_blk = pl.BlockSpec((1, 1, tm), lambda i, zs: (i, 0, 0), memory_space=pltpu.SMEM)
    in_specs = [smem_blk, smem_blk]
    first = 0
    for n in tiles:
        in_specs.append(pl.BlockSpec((tm * ROW_SUB, LANES),
                                     lambda i, zs, first=first, n=n: (jnp.clip(i - first, 0, n - 1), 0)))
        first += n
    grid_spec = pltpu.PrefetchScalarGridSpec(
        num_scalar_prefetch=1,
        grid=(sum(tiles),),
        in_specs=in_specs,
        out_specs=pl.BlockSpec(memory_space=pl.ANY),
        scratch_shapes=[pltpu.VMEM((BM_EXPERT * ROW_SUB, LANES), F32), pltpu.SemaphoreType.DMA(()),
                        pltpu.SemaphoreType.DMA(())],
    )
    return pl.pallas_call(
        functools.partial(_push_kernel, tiles=tiles),
        grid_spec=grid_spec,
        out_shape=jax.ShapeDtypeStruct((n_rows * ROW_SUB, LANES), F32),
        compiler_params=_cparams(1),
        name="row_push",
    )(zero_start, dest0, dest1, *xns)


def _expert_kernel(be_ref, nv_ref, x_ref, wg_ref, wu_ref, wd_ref, y_ref, wg_sc, wu_sc, wd_sc):
    i = pl.program_id(0)
    valid = i < nv_ref[0]
    new_expert = (i == 0) | (be_ref[i] != be_ref[jnp.maximum(i - 1, 0)])

    @pl.when(valid & new_expert)
    def _():
        wg_sc[...] = wg_ref[0].astype(BF16)
        wu_sc[...] = wu_ref[0].astype(BF16)
        wd_sc[...] = wd_ref[0].astype(BF16)

    @pl.when(valid)
    def _():
        xb = _load_row_tiles(x_ref).astype(BF16)
        gate = jnp.dot(xb, wg_sc[...], preferred_element_type=F32)
        up = jnp.dot(xb, wu_sc[...], preferred_element_type=F32)
        hb = (gate * jax.nn.sigmoid(gate) * up).astype(BF16)
        _store_row_tiles(y_ref, jnp.dot(hb, wd_sc[...], preferred_element_type=F32))

    @pl.when(jnp.logical_not(valid))
    def _():
        y_ref[...] = jnp.zeros_like(y_ref)


def _experts(block_e, n_valid, rows, w_gate, w_up, w_down):
    P = rows.shape[0] // ROW_SUB
    bm = BM_EXPERT
    last = lambda i, nv: jnp.minimum(i, jnp.maximum(nv[0] - 1, 0))
    row_blk = lambda i, be, nv: (last(i, nv), 0)
    w_blk = lambda i, be, nv: (be[last(i, nv)], 0, 0)
    grid_spec = pltpu.PrefetchScalarGridSpec(
        num_scalar_prefetch=2,
        grid=(P // bm,),
        in_specs=[
            pl.BlockSpec((bm * ROW_SUB, LANES), row_blk),
            pl.BlockSpec((1, D_MODEL, D_EXPERT), w_blk),
            pl.BlockSpec((1, D_MODEL, D_EXPERT), w_blk),
            pl.BlockSpec((1, D_EXPERT, D_MODEL), w_blk),
        ],
        out_specs=pl.BlockSpec((bm * ROW_SUB, LANES), lambda i, be, nv: (i, 0)),
        scratch_shapes=[pltpu.VMEM((D_MODEL, D_EXPERT), BF16), pltpu.VMEM((D_MODEL, D_EXPERT), BF16),
                        pltpu.VMEM((D_EXPERT, D_MODEL), BF16)],
    )
    return pl.pallas_call(
        _expert_kernel,
        grid_spec=grid_spec,
        out_shape=jax.ShapeDtypeStruct((P * ROW_SUB, LANES), F32),
        compiler_params=_cparams(1),
        name="experts",
    )(block_e, n_valid, rows, w_gate, w_up, w_down)


def _combine_kernel(d0_ref, d1_ref, d0n_ref, d1n_ref, x1_ref, ew_ref, gf_ref, y_ref, out_ref, buf, sem, *, n_tiles):
    tm = TM_ROWS
    i = pl.program_id(0)
    slot = i % 2

    def copies(d0, d1, sl, t):
        return (_row_copy(_row_tile(y_ref, d0[0, 0, t]), _row_tile(buf.at[sl, 0], t), sem.at[sl]),
                _row_copy(_row_tile(y_ref, d1[0, 0, t]), _row_tile(buf.at[sl, 1], t), sem.at[sl]))

    def gather_start(d0, d1, sl):
        def body(t, carry):
            c0, c1 = copies(d0, d1, sl, t)
            c0.start(priority=0)
            c1.start(priority=1)
            return carry
        lax.fori_loop(0, tm, body, 0, unroll=4)

    @pl.when(i == 0)
    def _():
        gather_start(d0_ref, d1_ref, 0)

    @pl.when(i + 1 < n_tiles)
    def _():
        gather_start(d0n_ref, d1n_ref, 1 - slot)

    def drain(t, carry):
        c0, c1 = copies(d0_ref, d1_ref, slot, t)
        c0.wait()
        c1.wait()
        return carry

    lax.fori_loop(0, tm, drain, 0, unroll=4)

    ew = ew_ref[...]
    x = (x1_ref[...] + _load_row_tiles(buf.at[slot, 0]) * ew[:, 0:1]
         + _load_row_tiles(buf.at[slot, 1]) * ew[:, 1:2])
    out_ref[...] = x * lax.rsqrt(jnp.mean(x * x, axis=-1, keepdims=True) + EPS) * gf_ref[...]


def _combine(dest0, dest1, x1, ew, final_g, yrows):
    T = x1.shape[0]
    tm = TM_ROWS
    nt = T // tm
    smem_blk = pl.BlockSpec((1, 1, tm), lambda i: (i, 0, 0), memory_space=pltpu.SMEM)
    smem_next = pl.BlockSpec((1, 1, tm), lambda i: (jnp.minimum(i + 1, nt - 1), 0, 0), memory_space=pltpu.SMEM)
    tok = lambda i: (i, 0)
    return pl.pallas_call(
        functools.partial(_combine_kernel, n_tiles=nt),
        grid=(nt,),
        in_specs=[smem_blk, smem_blk, smem_next, smem_next,
                  pl.BlockSpec((tm, D_MODEL), tok),
                  pl.BlockSpec((tm, LANES), tok),
                  pl.BlockSpec((1, D_MODEL), lambda i: (0, 0)),
                  pl.BlockSpec(memory_space=pl.ANY)],
        out_specs=pl.BlockSpec((tm, D_MODEL), tok),
        out_shape=jax.ShapeDtypeStruct((T, D_MODEL), F32),
        scratch_shapes=[pltpu.VMEM((2, 2, tm * ROW_SUB, LANES), F32), pltpu.SemaphoreType.DMA((2,))],
        compiler_params=_cparams(1),
        name="combine",
    )(dest0, dest1, dest0, dest1, x1, ew, final_g, yrows)


def _pool_bands():
    t = jnp.arange(POOL_BLOCK)[:, None]
    e = jnp.arange(POOL_BLOCK + 2 * POOL_HALO)[None, :]
    bands = []
    for kind in range(3):
        per_w = []
        for w in POOL_WINDOWS:
            half = w // 2
            win = (e >= t + POOL_HALO - half) & (e < t + POOL_HALO + half)
            lo_edge = jnp.maximum(t - half, 0) if kind == 0 else t - half
            hi_edge = jnp.minimum(t + half, POOL_BLOCK) if kind == 2 else t + half
            per_w.append(jnp.where(win, 1.0 / (hi_edge - lo_edge).astype(F32), 0.0))
        bands.append(jnp.stack(per_w))
    band = jnp.stack(bands)
    hi = band.astype(BF16)
    return hi, (band - hi.astype(F32)).astype(BF16)


def _prepare_params(norm1_g, w_in, pool_w, pool_scale, gate_b, head_norm_g, w_out, norm2_g,
                    router_group_w, router_group_b, router_expert_w, router_expert_b,
                    expert_w_gate, expert_w_up, expert_w_down, final_norm_g):
    w = w_in[0]
    n_main = 5 * 512
    w_main = jnp.concatenate([w[:, 0:512], w[:, 1024:1536]], axis=1).astype(BF16)
    w_feat_t = jnp.concatenate([w[:, 512:1024], w[:, 1536:n_main]], axis=1).T.astype(BF16)
    w_gate = jnp.pad(w[:, n_main:], ((0, 0), (0, LANES - N_GATES))).astype(BF16)
    gb = jnp.pad(gate_b[0], (0, LANES - N_GATES)).reshape(1, LANES)
    wr = jnp.zeros((ROUTER_ROWS, D_MODEL), F32)
    wr = wr.at[0:N_GROUPS].set(router_group_w[0].T)
    wr = wr.at[8:].set(router_expert_w[0].reshape(D_MODEL, N_EXPERTS).T)
    br = jnp.zeros((ROUTER_ROWS,), F32)
    br = br.at[0:N_GROUPS].set(router_group_b[0])
    br = br.at[8:].set(router_expert_b[0].reshape(N_EXPERTS))
    br = jnp.broadcast_to(br[:, None], (ROUTER_ROWS, LANES))
    idx = jnp.arange(CHUNK)
    tri_f = (idx[None, :] <= idx[:, None]).astype(F32)
    tri_b = (idx[None, :] >= idx[:, None]).astype(F32)
    tidx = jnp.arange(TM_MIX)
    su = (tidx[:, None] < tidx[None, :]).astype(BF16)
    band_hi, band_lo = _pool_bands()
    return dict(
        norm1_g=norm1_g[0].reshape(1, D_MODEL), w_main=w_main, w_feat_t=w_feat_t, w_gate=w_gate, gate_b=gb,
        pool_w=pool_w[0].astype(BF16), pool_scale=pool_scale[0].reshape(1, POOL_WIDTH),
        head_g_t=jnp.broadcast_to(head_norm_g[0][:, None], (MLSTM_WIDTH, LANES)), w_out=w_out[0].astype(BF16),
        norm2_g=norm2_g[0].reshape(1, D_MODEL), wr=wr.astype(BF16), br=br, tri_f=tri_f, tri_b=tri_b, su=su, band_hi=band_hi, band_lo=band_lo,
        e_gate=expert_w_gate[0], e_up=expert_w_up[0], e_down=expert_w_down[0],
        final_g=final_norm_g.reshape(1, D_MODEL),
    )


def _mixer(x, p, counts_in):
    B, S, _ = x.shape
    T = B * S
    x2 = x.reshape(T, D_MODEL)
    u, k, qt, vt, ot, gt = _inproj(x2, p["norm1_g"], p["w_main"], p["w_feat_t"], p["w_gate"], p["gate_b"])
    hft = _mlstm(k, qt, vt, gt, p["tri_f"], p["tri_b"], B, S, rev=False)
    m = _mlstm(k, qt, vt, gt, p["tri_b"], p["tri_f"], B, S, rev=True, hft=hft, ot=ot, head_g_t=p["head_g_t"])
    return _mix(x2, u, m, p["band_hi"], p["band_lo"], p["pool_w"], p["pool_scale"], p["w_out"], p["norm2_g"],
                p["wr"], p["br"], p["su"], counts_in, S)


def _encoders(xs, p):
    parts = []
    cnt = jnp.zeros((N_EXPERTS, LANES), F32)
    for x in xs:
        x1, xn, eid, rank, ew, cnt = _mixer(x, p, cnt)
        parts.append((x1, xn, eid, rank, ew))

    bm = BM_EXPERT
    n_tok = sum(part[0].shape[0] for part in parts)
    nb = (2 * n_tok) // bm + N_EXPERTS
    counts = cnt[:, 0].astype(I32)
    pcounts = ((counts + bm - 1) // bm) * bm
    pends = jnp.cumsum(pcounts)
    pstarts = pends - pcounts
    block_e = jnp.minimum(jnp.sum(pends[None, :] <= (jnp.arange(nb, dtype=I32) * bm)[:, None], axis=1),
                          N_EXPERTS - 1).astype(I32)
    n_valid = (pends[-1:] // bm).astype(I32)
    last_block = jnp.where(counts > 0, pends - bm, -1)
    tail_start = pends[-1] + jnp.arange(N_EXPERTS, dtype=I32) * bm
    tail_block = jnp.where(tail_start < nb * bm, tail_start, -1)
    zero_start = jnp.concatenate([last_block, tail_block]).astype(I32)
    e_ids = jnp.arange(N_EXPERTS, dtype=I32)[:, None, None]

    dests = []
    for x1, xn, eid, rank, ew in parts:
        nt = x1.shape[0] // TM_ROWS
        dest = rank + jnp.sum(jnp.where(eid[None] == e_ids, pstarts[:, None, None], 0), axis=0)
        dests.append((dest[0].reshape(nt, 1, TM_ROWS), dest[1].reshape(nt, 1, TM_ROWS)))
    rows = _push(zero_start, jnp.concatenate([d[0] for d in dests]), jnp.concatenate([d[1] for d in dests]),
                 [part[1] for part in parts], nb * bm)
    yrows = _experts(block_e, n_valid, rows, p["e_gate"], p["e_up"], p["e_down"])
    return tuple(_combine(d0, d1, x1, ew, p["final_g"], yrows).reshape(x.shape)
                 for x, (d0, d1), (x1, _, _, _, ew) in zip(xs, dests, parts))


def kernel(x_prompt, x_sample, norm1_g, w_in, pool_w, pool_scale, gate_b, head_norm_g, w_out, norm2_g,
           router_group_w, router_group_b, router_expert_w, router_expert_b,
           expert_w_gate, expert_w_up, expert_w_down, final_norm_g):
    p = _prepare_params(norm1_g, w_in, pool_w, pool_scale, gate_b, head_norm_g, w_out, norm2_g,
                        router_group_w, router_group_b, router_expert_w, router_expert_b,
                        expert_w_gate, expert_w_up, expert_w_down, final_norm_g)
    return _encoders((x_prompt, x_sample), p)
```

```python
import functools

import jax
import jax.numpy as jnp
from jax import lax
from jax.experimental import pallas as pl
from jax.experimental.pallas import tpu as pltpu

F32 = jnp.float32
BF16 = jnp.bfloat16
I32 = jnp.int32

D_MODEL = 1024
POOL_WIDTH = 512
POOL_WINDOWS = (2, 4, 8, 16)
POOL_GROUP = 128
POOL_HALO = 16
POOL_BLOCK = 128
MLSTM_WIDTH = 512
HEADS = 4
HEAD_DIM = 128
CHUNK = 128
N_GATES = 16
N_GROUPS = 4
EPG = 8
N_EXPERTS = 32
D_EXPERT = 512
EPS = 1e-6
LANES = 128
ROUTER_ROWS = 8 + N_EXPERTS

MLSTM_CHUNKS_PER_STEP = 16
STATE_ROWS = HEAD_DIM + 16
TM_PROJ = 1024
TM_MIX = 1024
TM_ROWS = 1024
ROW_SUB = D_MODEL // LANES
BM_EXPERT = 512
VMEM_LIMIT = 56 * 1024 * 1024

HIGHEST = lax.Precision.HIGHEST


def _cparams(n_axes):
    return pltpu.CompilerParams(dimension_semantics=("arbitrary",) * n_axes,
                                vmem_limit_bytes=VMEM_LIMIT)


def _store_row_tiles(ref, val):
    rows = val.shape[0]
    for j in range(ROW_SUB):
        ref[pl.ds(j, rows, stride=ROW_SUB), :] = val[:, j * LANES:(j + 1) * LANES]


def _load_row_tiles(ref):
    rows = ref.shape[0] // ROW_SUB
    return jnp.concatenate([ref[pl.ds(j, rows, stride=ROW_SUB), :] for j in range(ROW_SUB)], axis=1)


def _row_tile(ref, r):
    return ref.at[pl.ds(pl.multiple_of(r * ROW_SUB, ROW_SUB), ROW_SUB), :]


def _inproj_kernel(x_ref, g_ref, w_ref, wt_ref, wg_ref, gb_ref, u_ref, k_ref, qt_ref, vt_ref, ot_ref, gt_ref):
    x = x_ref[...]
    inv = lax.rsqrt(jnp.mean(x * x, axis=-1, keepdims=True) + EPS)
    xn = (x * inv * g_ref[...]).astype(BF16)

    def sec(i):
        return jnp.dot(xn, w_ref[:, i * 512:(i + 1) * 512], preferred_element_type=F32)

    def sec_t(i):
        return lax.dot_general(wt_ref[i * 512:(i + 1) * 512, :], xn, (((1,), (1,)), ((), ())),
                               preferred_element_type=F32)

    u_ref[...] = sec(0)
    k_ref[...] = (sec(1) * (HEAD_DIM ** -0.5)).astype(BF16)
    qt_ref[...] = sec_t(0).astype(BF16)
    vt_ref[...] = sec_t(1).astype(BF16)
    ot_ref[...] = sec_t(2)
    g = jnp.dot(xn, wg_ref[...], preferred_element_type=F32) + gb_ref[...]
    lane = lax.broadcasted_iota(I32, g.shape, 1)
    logsig = jnp.minimum(g, 0.0) - jnp.log1p(jnp.exp(-jnp.abs(g)))
    gt_ref[...] = jnp.where(lane >= 2 * HEADS, logsig, g)


def _inproj(x2, norm_g, w_main, w_feat_t, w_gate, gate_b):
    T = x2.shape[0]
    tm = TM_PROJ
    tok = lambda i: (i, 0)
    tok_t = lambda i: (0, i)
    fix = lambda i: (0, 0)
    outs = (
        jax.ShapeDtypeStruct((T, POOL_WIDTH), F32),
        jax.ShapeDtypeStruct((T, MLSTM_WIDTH), BF16),
        jax.ShapeDtypeStruct((MLSTM_WIDTH, T), BF16),
        jax.ShapeDtypeStruct((MLSTM_WIDTH, T), BF16),
        jax.ShapeDtypeStruct((MLSTM_WIDTH, T), F32),
        jax.ShapeDtypeStruct((T, LANES), F32),
    )
    return pl.pallas_call(
        _inproj_kernel,
        grid=(T // tm,),
        in_specs=[
            pl.BlockSpec((tm, D_MODEL), tok),
            pl.BlockSpec((1, D_MODEL), fix),
            pl.BlockSpec((D_MODEL, 2 * 512), fix),
            pl.BlockSpec((3 * 512, D_MODEL), fix),
            pl.BlockSpec((D_MODEL, LANES), fix),
            pl.BlockSpec((1, LANES), fix),
        ],
        out_specs=[
            pl.BlockSpec((tm, 512), tok), pl.BlockSpec((tm, 512), tok), pl.BlockSpec((512, tm), tok_t),
            pl.BlockSpec((512, tm), tok_t), pl.BlockSpec((512, tm), tok_t), pl.BlockSpec((tm, LANES), tok),
        ],
        out_shape=outs,
        compiler_params=_cparams(1),
        name="inproj",
    )(x2, norm_g, w_main, w_feat_t, w_gate, gate_b)


def _mlstm_kernel(*refs, rev):
    if rev:
        (k_ref, qt_ref, vt_ref, gt_ref, tri_ref, trit_ref, hft_ref, ot_ref, hgt_ref, out_ref,
         c_sc, m_sc, a_sc, upd_sc) = refs
    else:
        (k_ref, qt_ref, vt_ref, gt_ref, tri_ref, trit_ref, out_ref, c_sc, m_sc, a_sc, upd_sc) = refs
    L = CHUNK

    @pl.when(pl.program_id(1) == 0)
    def _():
        c_sc[...] = jnp.zeros_like(c_sc)
        m_sc[...] = jnp.zeros_like(m_sc)

    tri = tri_ref[...]
    mask_t = trit_ref[...] > 0.0
    sub8 = lax.broadcasted_iota(I32, (8, L), 0)
    sub_aug = lax.broadcasted_iota(I32, (STATE_ROWS - HEAD_DIM, L), 0)

    nch = MLSTM_CHUNKS_PER_STEP
    order = [nch - 1 - i if rev else i for i in range(nch)]
    hd = range(HEADS)
    cs = [HEADS * rev + h for h in hd]
    hsl = [slice(h * HEAD_DIM, (h + 1) * HEAD_DIM) for h in hd]
    rows = [slice(ci * L, (ci + 1) * L) for ci in range(nch)]
    pairs = [(ci, h) for ci in order for h in hd]
    row = lambda t, c: t[c:c + 1, :]
    last = 0 if rev else L - 1

    g = {ci: gt_ref[rows[ci], :] for ci in order}
    b_all = {ci: jnp.dot(tri, g[ci], precision=HIGHEST, preferred_element_type=F32) for ci in order}
    ba = {ci: pltpu.roll(b_all[ci], LANES - 2 * HEADS, axis=1) for ci in order}
    r_col = {ci: g[ci] - ba[ci] for ci in order}
    i_r = {ci: g[ci].T[0:8, :] for ci in order}
    b_r = {ci: ba[ci].T[0:8, :] for ci in order}
    b_last = {ci: b_r[ci][:, last:last + 1] for ci in order}
    logu = {ci: b_last[ci] - b_r[ci] + i_r[ci] for ci in order}
    mu = {ci: jnp.max(logu[ci], axis=1, keepdims=True) for ci in order}
    u_r = {ci: jnp.exp(logu[ci] - mu[ci]) for ci in order}

    kq = {p: jnp.dot(k_ref[rows[p[0]], hsl[p[1]]], qt_ref[hsl[p[1]], rows[p[0]]],
                     preferred_element_type=F32) for p in pairs}
    for p in pairs:
        ci, h = p
        u_row = row(u_r[ci], cs[h])
        vtu = vt_ref[hsl[h], rows[ci]].astype(F32) * u_row
        aug = jnp.concatenate([vtu, jnp.where(sub_aug == 0, u_row, 0.0)], axis=0).astype(BF16)
        upd_sc[ci * HEADS + h] = jnp.dot(aug, k_ref[rows[ci], hsl[h]], preferred_element_type=F32)
    logw = {p: jnp.where(mask_t, row(b_r[p[0]], cs[p[1]]) + r_col[p[0]][:, cs[p[1]]:cs[p[1]] + 1], -jnp.inf)
            for p in pairs}
    mw = {p: jnp.max(logw[p], axis=0, keepdims=True) for p in pairs}
    st = {p: kq[p] * jnp.exp(logw[p] - mw[p]) for p in pairs}
    dl = {p: jnp.sum(st[p], axis=0, keepdims=True) for p in pairs}
    for p in pairs:
        ci, h = p
        a_sc[ci * HEADS + h] = jnp.dot(vt_ref[hsl[h], rows[ci]], st[p].astype(BF16),
                                       preferred_element_type=F32)
    mw_r, dl_r = {}, {}
    for ci in order:
        t_mw = jnp.zeros((8, L), F32)
        t_dl = jnp.zeros((8, L), F32)
        for h in hd:
            t_mw = jnp.where(sub8 == cs[h], mw[(ci, h)], t_mw)
            t_dl = jnp.where(sub8 == cs[h], dl[(ci, h)], t_dl)
        mw_r[ci], dl_r[ci] = t_mw, t_dl

    for ci in order:
        m_prev = m_sc[...]
        m_new = jnp.maximum(b_last[ci] + m_prev, mu[ci])
        decay = jnp.exp(b_last[ci] + m_prev - m_new)
        u_scale = jnp.exp(mu[ci] - m_new)
        m_sc[...] = m_new
        inter_log = b_r[ci] + m_prev
        m_row = jnp.maximum(inter_log, mw_r[ci])
        w_intra = jnp.exp(mw_r[ci] - m_row)
        w_cross = jnp.exp(inter_log - m_row)
        c_old = [c_sc[h] for h in hd]
        x = [jnp.dot(c_old[h].astype(BF16), qt_ref[hsl[h], rows[ci]], preferred_element_type=F32)
             for h in hd]
        for h in hd:
            c_sc[h] = (decay[cs[h]:cs[h] + 1, 0:1] * c_old[h]
                       + u_scale[cs[h]:cs[h] + 1, 0:1] * upd_sc[ci * HEADS + h])
        dc_r = jnp.zeros((8, L), F32)
        for h in hd:
            dc_r = jnp.where(sub8 == cs[h], x[h][HEAD_DIM:HEAD_DIM + 1, :], dc_r)
        den = w_intra * dl_r[ci] + w_cross * dc_r
        rden = 1.0 / jnp.maximum(jnp.abs(den), jnp.exp(-m_row))
        f_intra = w_intra * rden
        f_cross = w_cross * rden
        ht = [row(f_intra, cs[h]) * a_sc[ci * HEADS + h] + row(f_cross, cs[h]) * x[h][0:HEAD_DIM, :] for h in hd]
        if rev:
            hsum = [hft_ref[hsl[h], rows[ci]] + ht[h] for h in hd]
            inv = [lax.rsqrt(jnp.mean(hsum[h] * hsum[h], axis=0, keepdims=True) + EPS) for h in hd]
            for h in hd:
                mt = jax.nn.sigmoid(ot_ref[hsl[h], rows[ci]]) * (hsum[h] * inv[h] * hgt_ref[hsl[h], :])
                out_ref[rows[ci], hsl[h]] = mt.T.astype(out_ref.dtype)
        else:
            for h in hd:
                out_ref[hsl[h], rows[ci]] = ht[h]


def _mlstm(k, qt, vt, gt, tri, tri_t, B, S, rev, hft=None, ot=None, head_g_t=None):
    T = B * S
    nrow = CHUNK * MLSTM_CHUNKS_PER_STEP
    nc = S // nrow
    if rev:
        step = lambda b, c: b * nc + (nc - 1 - c)
    else:
        step = lambda b, c: b * nc + c
    tok = lambda b, c: (step(b, c), 0)
    tok_t = lambda b, c: (0, step(b, c))
    fix = lambda b, c: (0, 0)
    blk = lambda w: pl.BlockSpec((nrow, w), tok)
    blk_t = pl.BlockSpec((MLSTM_WIDTH, nrow), tok_t)
    in_specs = [blk(512), blk_t, blk_t, blk(LANES), pl.BlockSpec((CHUNK, CHUNK), fix),
                pl.BlockSpec((CHUNK, CHUNK), fix)]
    args = [k, qt, vt, gt, tri, tri_t]
    if rev:
        in_specs += [blk_t, blk_t, pl.BlockSpec((MLSTM_WIDTH, LANES), fix)]
        args += [hft, ot, head_g_t]
        out_spec, out_shape = blk(512), jax.ShapeDtypeStruct((T, MLSTM_WIDTH), BF16)
    else:
        out_spec, out_shape = blk_t, jax.ShapeDtypeStruct((MLSTM_WIDTH, T), F32)
    return pl.pallas_call(
        functools.partial(_mlstm_kernel, rev=rev),
        grid=(B, nc),
        in_specs=in_specs,
        out_specs=out_spec,
        out_shape=out_shape,
        scratch_shapes=[pltpu.VMEM((HEADS, STATE_ROWS, HEAD_DIM), F32), pltpu.VMEM((8, LANES), F32),
                        pltpu.VMEM((MLSTM_CHUNKS_PER_STEP * HEADS, HEAD_DIM, CHUNK), F32),
                        pltpu.VMEM((MLSTM_CHUNKS_PER_STEP * HEADS, STATE_ROWS, HEAD_DIM), F32)],
        compiler_params=_cparams(2),
        name="mlstm_bwd" if rev else "mlstm_fwd",
    )(*args)


def _mix_kernel(x_ref, u_ref, up_ref, un_ref, m_ref, bh_ref, bl_ref, pw_ref, ps_ref, wo_ref, g2_ref, wr_ref, br_ref,
                su_ref, cin_ref, x1_ref, xn_ref, eid_ref, rank_ref, ew_ref, cnt_ref, ext_sc, run_sc, *, seq_len):
    tm = TM_MIX
    i = pl.program_id(0)
    tps = seq_len // tm
    it = i % tps

    @pl.when(i == 0)
    def _():
        run_sc[...] = cin_ref[...]

    ext_sc[0:POOL_HALO, :] = jnp.where(it == 0, 0.0, up_ref[...]).astype(BF16)
    ext_sc[POOL_HALO:POOL_HALO + tm, :] = u_ref[...].astype(BF16)
    ext_sc[POOL_HALO + tm:, :] = jnp.where(it == tps - 1, 0.0, un_ref[...]).astype(BF16)
    nblk = tm // POOL_BLOCK
    ngrp = len(POOL_WINDOWS)
    cells = [(b, gi) for b in range(nblk) for gi in range(ngrp)]
    rs = [slice(b * POOL_BLOCK, (b + 1) * POOL_BLOCK) for b in range(nblk)]
    ls = [slice(gi * POOL_GROUP, (gi + 1) * POOL_GROUP) for gi in range(ngrp)]
    first = jnp.where(it == 0, 0, 1)
    final = jnp.where(it == tps - 1, 2, 1)

    def mean(b, gi):
        slab = ext_sc[b * POOL_BLOCK:(b + 1) * POOL_BLOCK + 2 * POOL_HALO, ls[gi]]
        if 0 < b < nblk - 1:
            return jnp.dot(bh_ref[1, gi], slab, preferred_element_type=F32)
        kind = first if b == 0 else final
        return (jnp.dot(bh_ref[kind, gi], slab, preferred_element_type=F32)
                + jnp.dot(bl_ref[kind, gi], slab, preferred_element_type=F32))

    means = {c: mean(*c) for c in cells}
    dev = {c: (means[c] - u_ref[rs[c[0]], ls[c[1]]]).astype(BF16) for c in cells}
    lin = {c: jnp.dot(dev[c], pw_ref[c[1]], preferred_element_type=F32) for c in cells}
    a = jnp.concatenate(
        [(jnp.concatenate([lin[(b, gi)] for gi in range(ngrp)], axis=1) * ps_ref[...]).astype(BF16)
         for b in range(nblk)], axis=0)

    x1 = (x_ref[...]
          + jnp.dot(a, wo_ref[0:POOL_WIDTH, :], preferred_element_type=F32)
          + jnp.dot(m_ref[...], wo_ref[POOL_WIDTH:, :], preferred_element_type=F32))
    x1_ref[...] = x1
    xn = x1 * lax.rsqrt(jnp.mean(x1 * x1, axis=-1, keepdims=True) + EPS) * g2_ref[...]
    _store_row_tiles(xn_ref, xn)

    lg = lax.dot_general(wr_ref[...], xn.astype(BF16), (((1,), (1,)), ((), ())),
                         preferred_element_type=F32) + br_ref[:, 0:1]
    gl = lg[0:N_GROUPS, :]
    gmax = jnp.max(gl, axis=0, keepdims=True)
    giota = lax.broadcasted_iota(I32, gl.shape, 0).astype(F32)
    g_idx = jnp.min(jnp.where(gl == gmax, giota, float(N_GROUPS)), axis=0, keepdims=True)
    g_w = 1.0 / jnp.sum(jnp.exp(gl - gmax), axis=0, keepdims=True)
    el = lg[8:8 + EPG, :]
    for gg in range(1, N_GROUPS):
        el = jnp.where(g_idx == float(gg), lg[8 + gg * EPG:8 + (gg + 1) * EPG, :], el)
    eiota = lax.broadcasted_iota(I32, el.shape, 0).astype(F32)
    top1 = jnp.max(el, axis=0, keepdims=True)
    i1 = jnp.min(jnp.where(el == top1, eiota, float(EPG)), axis=0, keepdims=True)
    el2 = jnp.where(eiota == i1, -jnp.inf, el)
    top2 = jnp.max(el2, axis=0, keepdims=True)
    i2 = jnp.min(jnp.where(el2 == top2, eiota, float(EPG)), axis=0, keepdims=True)
    ex = jnp.exp(top2 - top1)
    w1 = g_w / (1.0 + ex)
    w2 = g_w * ex / (1.0 + ex)
    e1 = g_idx * float(EPG) + i1
    e2 = g_idx * float(EPG) + i2
    eid_ref[0:1, :] = e1.astype(I32)
    eid_ref[1:2, :] = e2.astype(I32)

    xiota = lax.broadcasted_iota(I32, (N_EXPERTS, tm), 0).astype(F32)
    oh1 = xiota == e1
    oh2 = xiota == e2
    oh = jnp.where(oh1 | oh2, 1.0, 0.0)
    before = jnp.dot(oh.astype(BF16), su_ref[...], preferred_element_type=F32)
    base = run_sc[:, 0:1] + before
    rank_ref[0:1, :] = jnp.sum(jnp.where(oh1, base, 0.0), axis=0, keepdims=True).astype(I32)
    rank_ref[1:2, :] = jnp.sum(jnp.where(oh2, base, 0.0), axis=0, keepdims=True).astype(I32)
    run_new = run_sc[...] + jnp.sum(oh, axis=1, keepdims=True)
    run_sc[...] = run_new
    cnt_ref[...] = run_new

    riota = lax.broadcasted_iota(I32, (LANES, tm), 0)
    wt = jnp.where(riota == 0, w1, jnp.where(riota == 1, w2, 0.0))
    ew_ref[...] = wt.T


def _mix(x2, u, m, band_hi, band_lo, pool_w, pool_scale, w_out, norm2_g, wr, br, su, counts_in, seq_len):
    T = x2.shape[0]
    tm = TM_MIX
    assert seq_len % tm == 0 and seq_len >= 2 * POOL_BLOCK
    hb = tm // POOL_HALO
    nhb = T // POOL_HALO
    band_spec = pl.BlockSpec((3, len(POOL_WINDOWS), POOL_BLOCK, POOL_BLOCK + 2 * POOL_HALO), lambda i: (0, 0, 0, 0))
    tok = lambda i: (i, 0)
    fix = lambda i: (0, 0)
    outs = (
        jax.ShapeDtypeStruct((T, D_MODEL), F32),
        jax.ShapeDtypeStruct((T * ROW_SUB, LANES), F32),
        jax.ShapeDtypeStruct((2, T), I32),
        jax.ShapeDtypeStruct((2, T), I32),
        jax.ShapeDtypeStruct((T, LANES), F32),
        jax.ShapeDtypeStruct((N_EXPERTS, LANES), F32),
    )
    return pl.pallas_call(
        functools.partial(_mix_kernel, seq_len=seq_len),
        grid=(T // tm,),
        in_specs=[
            pl.BlockSpec((tm, D_MODEL), tok),
            pl.BlockSpec((tm, POOL_WIDTH), tok),
            pl.BlockSpec((POOL_HALO, POOL_WIDTH), lambda i: (jnp.maximum(i * hb - 1, 0), 0)),
            pl.BlockSpec((POOL_HALO, POOL_WIDTH), lambda i: (jnp.minimum((i + 1) * hb, nhb - 1), 0)),
            pl.BlockSpec((tm, MLSTM_WIDTH), tok),
            band_spec,
            band_spec,
            pl.BlockSpec((len(POOL_WINDOWS), POOL_GROUP, POOL_GROUP), lambda i: (0, 0, 0)),
            pl.BlockSpec((1, POOL_WIDTH), fix),
            pl.BlockSpec((D_MODEL, D_MODEL), fix),
            pl.BlockSpec((1, D_MODEL), fix),
            pl.BlockSpec((ROUTER_ROWS, D_MODEL), fix),
            pl.BlockSpec((ROUTER_ROWS, LANES), fix),
            pl.BlockSpec((tm, tm), fix),
            pl.BlockSpec((N_EXPERTS, LANES), fix),
        ],
        out_specs=[
            pl.BlockSpec((tm, D_MODEL), tok),
            pl.BlockSpec((tm * ROW_SUB, LANES), tok),
            pl.BlockSpec((2, tm), lambda i: (0, i)),
            pl.BlockSpec((2, tm), lambda i: (0, i)),
            pl.BlockSpec((tm, LANES), tok),
            pl.BlockSpec((N_EXPERTS, LANES), fix),
        ],
        out_shape=outs,
        scratch_shapes=[pltpu.VMEM((tm + 2 * POOL_HALO, POOL_WIDTH), BF16), pltpu.VMEM((N_EXPERTS, LANES), F32)],
        compiler_params=_cparams(1),
        name="mix_router",
    )(x2, u, u, u, m, band_hi, band_lo, pool_w, pool_scale, w_out, norm2_g, wr, br, su, counts_in)


def _row_copy(src, dst, sem):
    return pltpu.make_async_copy(src, dst, sem)


def _push_kernel(zs_ref, d0_ref, d1_ref, *refs, tiles):
    xn_refs = refs[:len(tiles)]
    rows_ref, zero_sc, sem, zsem = refs[len(tiles):]
    tm = TM_ROWS
    i = pl.program_id(0)

    @pl.when(i == 0)
    def _():
        zero_sc[...] = jnp.zeros_like(zero_sc)

        def fill(e):
            start = pl.multiple_of(jnp.maximum(zs_ref[e], 0) * ROW_SUB, ROW_SUB)
            return _row_copy(zero_sc, rows_ref.at[pl.ds(start, BM_EXPERT * ROW_SUB), :], zsem)

        for e in range(2 * N_EXPERTS):
            @pl.when(zs_ref[e] >= 0)
            def _():
                fill(e).start()
        for e in range(2 * N_EXPERTS):
            @pl.when(zs_ref[e] >= 0)
            def _():
                fill(e).wait()

    def push_tile(xn_ref):
        def copies(t):
            src = _row_tile(xn_ref, t)
            return (_row_copy(src, _row_tile(rows_ref, d0_ref[0, 0, t]), sem),
                    _row_copy(src, _row_tile(rows_ref, d1_ref[0, 0, t]), sem))

        def issue(t, carry):
            c0, c1 = copies(t)
            c0.start(priority=0)
            c1.start(priority=1)
            return carry

        def drain(t, carry):
            c0, c1 = copies(t)
            c0.wait()
            c1.wait()
            return carry

        lax.fori_loop(0, tm, issue, 0, unroll=4)
        lax.fori_loop(0, tm, drain, 0, unroll=4)

    first = 0
    for xn_ref, n in zip(xn_refs, tiles):
        @pl.when((i >= first) & (i < first + n))
        def _():
            push_tile(xn_ref)
        first += n


def _push(zero_start, dest0, dest1, xns, n_rows):
    tm = TM_ROWS
    tiles = tuple(xn.shape[0] // (ROW_SUB * tm) for xn in xns)
    smem_blk = pl.BlockSpec((1, 1, tm), lambda i, zs: (i, 0, 0), memory_space=pltpu.SMEM)
    in_specs = [smem_blk, smem_blk]
    first = 0
    for n in tiles:
        in_specs.append(pl.BlockSpec((tm * ROW_SUB, LANES),
                                     lambda i, zs, first=first, n=n: (jnp.clip(i - first, 0, n - 1), 0)))
        first += n
    grid_spec = pltpu.PrefetchScalarGridSpec(
        num_scalar_prefetch=1,
        grid=(sum(tiles),),
        in_specs=in_specs,
        out_specs=pl.BlockSpec(memory_space=pl.ANY),
        scratch_shapes=[pltpu.VMEM((BM_EXPERT * ROW_SUB, LANES), F32), pltpu.SemaphoreType.DMA(()),
                        pltpu.SemaphoreType.DMA(())],
    )
    return pl.pallas_call(
        functools.partial(_push_kernel, tiles=tiles),
        grid_spec=grid_spec,
        out_shape=jax.ShapeDtypeStruct((n_rows * ROW_SUB, LANES), F32),
        compiler_params=_cparams(1),
        name="row_push",
    )(zero_start, dest0, dest1, *xns)


def _expert_kernel(be_ref, nv_ref, x_ref, wg_ref, wu_ref, wd_ref, y_ref, wg_sc, wu_sc, wd_sc):
    i = pl.program_id(0)
    valid = i < nv_ref[0]
    new_expert = (i == 0) | (be_ref[i] != be_ref[jnp.maximum(i - 1, 0)])

    @pl.when(valid & new_expert)
    def _():
        wg_sc[...] = wg_ref[0].astype(BF16)
        wu_sc[...] = wu_ref[0].astype(BF16)
        wd_sc[...] = wd_ref[0].astype(BF16)

    @pl.when(valid)
    def _():
        xb = _load_row_tiles(x_ref).astype(BF16)
        gate = jnp.dot(xb, wg_sc[...], preferred_element_type=F32)
        up = jnp.dot(xb, wu_sc[...], preferred_element_type=F32)
        hb = (gate * jax.nn.sigmoid(gate) * up).astype(BF16)
        _store_row_tiles(y_ref, jnp.dot(hb, wd_sc[...], preferred_element_type=F32))

    @pl.when(jnp.logical_not(valid))
    def _():
        y_ref[...] = jnp.zeros_like(y_ref)


def _experts(block_e, n_valid, rows, w_gate, w_up, w_down):
    P = rows.shape[0] // ROW_SUB
    bm = BM_EXPERT
    last = lambda i, nv: jnp.minimum(i, jnp.maximum(nv[0] - 1, 0))
    row_blk = lambda i, be, nv: (last(i, nv), 0)
    w_blk = lambda i, be, nv: (be[last(i, nv)], 0, 0)
    grid_spec = pltpu.PrefetchScalarGridSpec(
        num_scalar_prefetch=2,
        grid=(P // bm,),
        in_specs=[
            pl.BlockSpec((bm * ROW_SUB, LANES), row_blk),
            pl.BlockSpec((1, D_MODEL, D_EXPERT), w_blk),
            pl.BlockSpec((1, D_MODEL, D_EXPERT), w_blk),
            pl.BlockSpec((1, D_EXPERT, D_MODEL), w_blk),
        ],
        out_specs=pl.BlockSpec((bm * ROW_SUB, LANES), lambda i, be, nv: (i, 0)),
        scratch_shapes=[pltpu.VMEM((D_MODEL, D_EXPERT), BF16), pltpu.VMEM((D_MODEL, D_EXPERT), BF16),
                        pltpu.VMEM((D_EXPERT, D_MODEL), BF16)],
    )
    return pl.pallas_call(
        _expert_kernel,
        grid_spec=grid_spec,
        out_shape=jax.ShapeDtypeStruct((P * ROW_SUB, LANES), F32),
        compiler_params=_cparams(1),
        name="experts",
    )(block_e, n_valid, rows, w_gate, w_up, w_down)


def _combine_kernel(d0_ref, d1_ref, d0n_ref, d1n_ref, x1_ref, ew_ref, gf_ref, y_ref, out_ref, buf, sem, *, n_tiles):
    tm = TM_ROWS
    i = pl.program_id(0)
    slot = i % 2

    def copies(d0, d1, sl, t):
        return (_row_copy(_row_tile(y_ref, d0[0, 0, t]), _row_tile(buf.at[sl, 0], t), sem.at[sl]),
                _row_copy(_row_tile(y_ref, d1[0, 0, t]), _row_tile(buf.at[sl, 1], t), sem.at[sl]))

    def gather_start(d0, d1, sl):
        def body(t, carry):
            c0, c1 = copies(d0, d1, sl, t)
            c0.start(priority=0)
            c1.start(priority=1)
            return carry
        lax.fori_loop(0, tm, body, 0, unroll=4)

    @pl.when(i == 0)
    def _():
        gather_start(d0_ref, d1_ref, 0)

    @pl.when(i + 1 < n_tiles)
    def _():
        gather_start(d0n_ref, d1n_ref, 1 - slot)

    def drain(t, carry):
        c0, c1 = copies(d0_ref, d1_ref, slot, t)
        c0.wait()
        c1.wait()
        return carry

    lax.fori_loop(0, tm, drain, 0, unroll=4)

    ew = ew_ref[...]
    x = (x1_ref[...] + _load_row_tiles(buf.at[slot, 0]) * ew[:, 0:1]
         + _load_row_tiles(buf.at[slot, 1]) * ew[:, 1:2])
    out_ref[...] = x * lax.rsqrt(jnp.mean(x * x, axis=-1, keepdims=True) + EPS) * gf_ref[...]


def _combine(dest0, dest1, x1, ew, final_g, yrows):
    T = x1.shape[0]
    tm = TM_ROWS
    nt = T // tm
    smem_blk = pl.BlockSpec((1, 1, tm), lambda i: (i, 0, 0), memory_space=pltpu.SMEM)
    smem_next = pl.BlockSpec((1, 1, tm), lambda i: (jnp.minimum(i + 1, nt - 1), 0, 0), memory_space=pltpu.SMEM)
    tok = lambda i: (i, 0)
    return pl.pallas_call(
        functools.partial(_combine_kernel, n_tiles=nt),
        grid=(nt,),
        in_specs=[smem_blk, smem_blk, smem_next, smem_next,
                  pl.BlockSpec((tm, D_MODEL), tok),
                  pl.BlockSpec((tm, LANES), tok),
                  pl.BlockSpec((1, D_MODEL), lambda i: (0, 0)),
                  pl.BlockSpec(memory_space=pl.ANY)],
        out_specs=pl.BlockSpec((tm, D_MODEL), tok),
        out_shape=jax.ShapeDtypeStruct((T, D_MODEL), F32),
        scratch_shapes=[pltpu.VMEM((2, 2, tm * ROW_SUB, LANES), F32), pltpu.SemaphoreType.DMA((2,))],
        compiler_params=_cparams(1),
        name="combine",
    )(dest0, dest1, dest0, dest1, x1, ew, final_g, yrows)


def _pool_bands():
    t = jnp.arange(POOL_BLOCK)[:, None]
    e = jnp.arange(POOL_BLOCK + 2 * POOL_HALO)[None, :]
    bands = []
    for kind in range(3):
        per_w = []
        for w in POOL_WINDOWS:
            half = w // 2
            win = (e >= t + POOL_HALO - half) & (e < t + POOL_HALO + half)
            lo_edge = jnp.maximum(t - half, 0) if kind == 0 else t - half
            hi_edge = jnp.minimum(t + half, POOL_BLOCK) if kind == 2 else t + half
            per_w.append(jnp.where(win, 1.0 / (hi_edge - lo_edge).astype(F32), 0.0))
        bands.append(jnp.stack(per_w))
    band = jnp.stack(bands)
    hi = band.astype(BF16)
    return hi, (band - hi.astype(F32)).astype(BF16)


def _prepare_params(norm1_g, w_in, pool_w, pool_scale, gate_b, head_norm_g, w_out, norm2_g,
                    router_group_w, router_group_b, router_expert_w, router_expert_b,
                    expert_w_gate, expert_w_up, expert_w_down, final_norm_g):
    w = w_in[0]
    n_main = 5 * 512
    w_main = jnp.concatenate([w[:, 0:512], w[:, 1024:1536]], axis=1).astype(BF16)
    w_feat_t = jnp.concatenate([w[:, 512:1024], w[:, 1536:n_main]], axis=1).T.astype(BF16)
    w_gate = jnp.pad(w[:, n_main:], ((0, 0), (0, LANES - N_GATES))).astype(BF16)
    gb = jnp.pad(gate_b[0], (0, LANES - N_GATES)).reshape(1, LANES)
    wr = jnp.zeros((ROUTER_ROWS, D_MODEL), F32)
    wr = wr.at[0:N_GROUPS].set(router_group_w[0].T)
    wr = wr.at[8:].set(router_expert_w[0].reshape(D_MODEL, N_EXPERTS).T)
    br = jnp.zeros((ROUTER_ROWS,), F32)
    br = br.at[0:N_GROUPS].set(router_group_b[0])
    br = br.at[8:].set(router_expert_b[0].reshape(N_EXPERTS))
    br = jnp.broadcast_to(br[:, None], (ROUTER_ROWS, LANES))
    idx = jnp.arange(CHUNK)
    tri_f = (idx[None, :] <= idx[:, None]).astype(F32)
    tri_b = (idx[None, :] >= idx[:, None]).astype(F32)
    tidx = jnp.arange(TM_MIX)
    su = (tidx[:, None] < tidx[None, :]).astype(BF16)
    band_hi, band_lo = _pool_bands()
    return dict(
        norm1_g=norm1_g[0].reshape(1, D_MODEL), w_main=w_main, w_feat_t=w_feat_t, w_gate=w_gate, gate_b=gb,
        pool_w=pool_w[0].astype(BF16), pool_scale=pool_scale[0].reshape(1, POOL_WIDTH),
        head_g_t=jnp.broadcast_to(head_norm_g[0][:, None], (MLSTM_WIDTH, LANES)), w_out=w_out[0].astype(BF16),
        norm2_g=norm2_g[0].reshape(1, D_MODEL), wr=wr.astype(BF16), br=br, tri_f=tri_f, tri_b=tri_b, su=su, band_hi=band_hi, band_lo=band_lo,
        e_gate=expert_w_gate[0], e_up=expert_w_up[0], e_down=expert_w_down[0],
        final_g=final_norm_g.reshape(1, D_MODEL),
    )


def _mixer(x, p, counts_in):
    B, S, _ = x.shape
    T = B * S
    x2 = x.reshape(T, D_MODEL)
    u, k, qt, vt, ot, gt = _inproj(x2, p["norm1_g"], p["w_main"], p["w_feat_t"], p["w_gate"], p["gate_b"])
    hft = _mlstm(k, qt, vt, gt, p["tri_f"], p["tri_b"], B, S, rev=False)
    m = _mlstm(k, qt, vt, gt, p["tri_b"], p["tri_f"], B, S, rev=True, hft=hft, ot=ot, head_g_t=p["head_g_t"])
    return _mix(x2, u, m, p["band_hi"], p["band_lo"], p["pool_w"], p["pool_scale"], p["w_out"], p["norm2_g"],
                p["wr"], p["br"], p["su"], counts_in, S)


def _encoders(xs, p):
    parts = []
    cnt = jnp.zeros((N_EXPERTS, LANES), F32)
    for x in xs:
        x1, xn, eid, rank, ew, cnt = _mixer(x, p, cnt)
        parts.append((x1, xn, eid, rank, ew))

    bm = BM_EXPERT
    n_tok = sum(part[0].shape[0] for part in parts)
    nb = (2 * n_tok) // bm + N_EXPERTS
    counts = cnt[:, 0].astype(I32)
    pcounts = ((counts + bm - 1) // bm) * bm
    pends = jnp.cumsum(pcounts)
    pstarts = pends - pcounts
    block_e = jnp.minimum(jnp.sum(pends[None, :] <= (jnp.arange(nb, dtype=I32) * bm)[:, None], axis=1),
                          N_EXPERTS - 1).astype(I32)
    n_valid = (pends[-1:] // bm).astype(I32)
    last_block = jnp.where(counts > 0, pends - bm, -1)
    tail_start = pends[-1] + jnp.arange(N_EXPERTS, dtype=I32) * bm
    tail_block = jnp.where(tail_start < nb * bm, tail_start, -1)
    zero_start = jnp.concatenate([last_block, tail_block]).astype(I32)
    e_ids = jnp.arange(N_EXPERTS, dtype=I32)[:, None, None]

    dests = []
    for x1, xn, eid, rank, ew in parts:
        nt = x1.shape[0] // TM_ROWS
        dest = rank + jnp.sum(jnp.where(eid[None] == e_ids, pstarts[:, None, None], 0), axis=0)
        dests.append((dest[0].reshape(nt, 1, TM_ROWS), dest[1].reshape(nt, 1, TM_ROWS)))
    rows = _push(zero_start, jnp.concatenate([d[0] for d in dests]), jnp.concatenate([d[1] for d in dests]),
                 [part[1] for part in parts], nb * bm)
    yrows = _experts(block_e, n_valid, rows, p["e_gate"], p["e_up"], p["e_down"])
    return tuple(_combine(d0, d1, x1, ew, p["final_g"], yrows).reshape(x.shape)
                 for x, (d0, d1), (x1, _, _, _, ew) in zip(xs, dests, parts))


def kernel(x_prompt, x_sample, norm1_g, w_in, pool_w, pool_scale, gate_b, head_norm_g, w_out, norm2_g,
           router_group_w, router_group_b, router_expert_w, router_expert_b,
           expert_w_gate, expert_w_up, expert_w_down, final_norm_g):
    p = _prepare_params(norm1_g, w_in, pool_w, pool_scale, gate_b, head_norm_g, w_out, norm2_g,
                        router_group_w, router_group_b, router_expert_w, router_expert_b,
                        expert_w_gate, expert_w_up, expert_w_down, final_norm_g)
    return _encoders((x_prompt, x_sample), p)
```

```python
import functools

import jax
import jax.numpy as jnp
from jax import lax
from jax.experimental import pallas as pl
from jax.experimental.pallas import tpu as pltpu

F32 = jnp.float32
BF16 = jnp.bfloat16
I32 = jnp.int32

D_MODEL = 1024
POOL_WIDTH = 512
POOL_WINDOWS = (2, 4, 8, 16)
POOL_GROUP = 128
POOL_HALO = 16
POOL_BLOCK = 128
MLSTM_WIDTH = 512
HEADS = 4
HEAD_DIM = 128
CHUNK = 128
N_GATES = 16
N_GROUPS = 4
EPG = 8
N_EXPERTS = 32
D_EXPERT = 512
EPS = 1e-6
LANES = 128
ROUTER_ROWS = 8 + N_EXPERTS

MLSTM_CHUNKS_PER_STEP = 16
STATE_ROWS = HEAD_DIM + 16
TM_PROJ = 1024
TM_MIX = 1024
TM_PUSH = 1024
TM_GATHER = 256
ROW_SUB = D_MODEL // LANES
BM_EXPERT = 512
VMEM_LIMIT = 56 * 1024 * 1024

HIGHEST = lax.Precision.HIGHEST


def _cparams(n_axes):
    return pltpu.CompilerParams(dimension_semantics=("arbitrary",) * n_axes,
                                vmem_limit_bytes=VMEM_LIMIT)


def _store_row_tiles(ref, val):
    rows = val.shape[0]
    for j in range(ROW_SUB):
        ref[pl.ds(j, rows, stride=ROW_SUB), :] = val[:, j * LANES:(j + 1) * LANES]


def _load_row_tiles(ref):
    rows = ref.shape[0] // ROW_SUB
    return jnp.concatenate([ref[pl.ds(j, rows, stride=ROW_SUB), :] for j in range(ROW_SUB)], axis=1)


def _row_tile(ref, r):
    return ref.at[pl.ds(pl.multiple_of(r * ROW_SUB, ROW_SUB), ROW_SUB), :]


def _inproj_kernel(x_ref, g_ref, w_ref, wt_ref, wg_ref, gb_ref, u_ref, k_ref, qt_ref, vt_ref, ot_ref, gt_ref):
    x = x_ref[...]
    inv = lax.rsqrt(jnp.mean(x * x, axis=-1, keepdims=True) + EPS)
    xn = (x * inv * g_ref[...]).astype(BF16)

    def sec(i):
        return jnp.dot(xn, w_ref[:, i * 512:(i + 1) * 512], preferred_element_type=F32)

    def sec_t(i):
        return lax.dot_general(wt_ref[i * 512:(i + 1) * 512, :], xn, (((1,), (1,)), ((), ())),
                               preferred_element_type=F32)

    u_ref[...] = sec(0)
    k_ref[...] = (sec(1) * (HEAD_DIM ** -0.5)).astype(BF16)
    qt_ref[...] = sec_t(0).astype(BF16)
    vt_ref[...] = sec_t(1).astype(BF16)
    ot_ref[...] = sec_t(2)
    g = jnp.dot(xn, wg_ref[...], preferred_element_type=F32) + gb_ref[...]
    lane = lax.broadcasted_iota(I32, g.shape, 1)
    logsig = jnp.minimum(g, 0.0) - jnp.log1p(jnp.exp(-jnp.abs(g)))
    gt_ref[...] = jnp.where(lane >= 2 * HEADS, logsig, g)


def _inproj(x2, norm_g, w_main, w_feat_t, w_gate, gate_b):
    T = x2.shape[0]
    tm = TM_PROJ
    tok = lambda i: (i, 0)
    tok_t = lambda i: (0, i)
    fix = lambda i: (0, 0)
    outs = (
        jax.ShapeDtypeStruct((T, POOL_WIDTH), F32),
        jax.ShapeDtypeStruct((T, MLSTM_WIDTH), BF16),
        jax.ShapeDtypeStruct((MLSTM_WIDTH, T), BF16),
        jax.ShapeDtypeStruct((MLSTM_WIDTH, T), BF16),
        jax.ShapeDtypeStruct((MLSTM_WIDTH, T), F32),
        jax.ShapeDtypeStruct((T, LANES), F32),
    )
    return pl.pallas_call(
        _inproj_kernel,
        grid=(T // tm,),
        in_specs=[
            pl.BlockSpec((tm, D_MODEL), tok),
            pl.BlockSpec((1, D_MODEL), fix),
            pl.BlockSpec((D_MODEL, 2 * 512), fix),
            pl.BlockSpec((3 * 512, D_MODEL), fix),
            pl.BlockSpec((D_MODEL, LANES), fix),
            pl.BlockSpec((1, LANES), fix),
        ],
        out_specs=[
            pl.BlockSpec((tm, 512), tok), pl.BlockSpec((tm, 512), tok), pl.BlockSpec((512, tm), tok_t),
            pl.BlockSpec((512, tm), tok_t), pl.BlockSpec((512, tm), tok_t), pl.BlockSpec((tm, LANES), tok),
        ],
        out_shape=outs,
        compiler_params=_cparams(1),
        name="inproj",
    )(x2, norm_g, w_main, w_feat_t, w_gate, gate_b)


def _mlstm_kernel(*refs, rev):
    if rev:
        (k_ref, qt_ref, vt_ref, gt_ref, tri_ref, trit_ref, hft_ref, ot_ref, hgt_ref, out_ref,
         c_sc, m_sc, a_sc, upd_sc) = refs
    else:
        (k_ref, qt_ref, vt_ref, gt_ref, tri_ref, trit_ref, out_ref, c_sc, m_sc, a_sc, upd_sc) = refs
    L = CHUNK

    @pl.when(pl.program_id(1) == 0)
    def _():
        c_sc[...] = jnp.zeros_like(c_sc)
        m_sc[...] = jnp.zeros_like(m_sc)

    tri = tri_ref[...]
    mask_t = trit_ref[...] > 0.0
    sub8 = lax.broadcasted_iota(I32, (8, L), 0)
    sub_aug = lax.broadcasted_iota(I32, (STATE_ROWS - HEAD_DIM, L), 0)

    nch = MLSTM_CHUNKS_PER_STEP
    order = [nch - 1 - i if rev else i for i in range(nch)]
    hd = range(HEADS)
    cs = [HEADS * rev + h for h in hd]
    hsl = [slice(h * HEAD_DIM, (h + 1) * HEAD_DIM) for h in hd]
    rows = [slice(ci * L, (ci + 1) * L) for ci in range(nch)]
    pairs = [(ci, h) for ci in order for h in hd]
    row = lambda t, c: t[c:c + 1, :]
    last = 0 if rev else L - 1

    g = {ci: gt_ref[rows[ci], :] for ci in order}
    b_all = {ci: jnp.dot(tri, g[ci], precision=HIGHEST, preferred_element_type=F32) for ci in order}
    ba = {ci: pltpu.roll(b_all[ci], LANES - 2 * HEADS, axis=1) for ci in order}
    r_col = {ci: g[ci] - ba[ci] for ci in order}
    i_r = {ci: g[ci].T[0:8, :] for ci in order}
    b_r = {ci: ba[ci].T[0:8, :] for ci in order}
    b_last = {ci: b_r[ci][:, last:last + 1] for ci in order}
    logu = {ci: b_last[ci] - b_r[ci] + i_r[ci] for ci in order}
    mu = {ci: jnp.max(logu[ci], axis=1, keepdims=True) for ci in order}
    u_r = {ci: jnp.exp(logu[ci] - mu[ci]) for ci in order}

    kq = {p: jnp.dot(k_ref[rows[p[0]], hsl[p[1]]], qt_ref[hsl[p[1]], rows[p[0]]],
                     preferred_element_type=F32) for p in pairs}
    for p in pairs:
        ci, h = p
        u_row = row(u_r[ci], cs[h])
        vtu = vt_ref[hsl[h], rows[ci]].astype(F32) * u_row
        aug = jnp.concatenate([vtu, jnp.where(sub_aug == 0, u_row, 0.0)], axis=0).astype(BF16)
        upd_sc[ci * HEADS + h] = jnp.dot(aug, k_ref[rows[ci], hsl[h]], preferred_element_type=F32)
    logw = {p: jnp.where(mask_t, row(b_r[p[0]], cs[p[1]]) + r_col[p[0]][:, cs[p[1]]:cs[p[1]] + 1], -jnp.inf)
            for p in pairs}
    mw = {p: jnp.max(logw[p], axis=0, keepdims=True) for p in pairs}
    st = {p: kq[p] * jnp.exp(logw[p] - mw[p]) for p in pairs}
    dl = {p: jnp.sum(st[p], axis=0, keepdims=True) for p in pairs}
    for p in pairs:
        ci, h = p
        a_sc[ci * HEADS + h] = jnp.dot(vt_ref[hsl[h], rows[ci]], st[p].astype(BF16),
                                       preferred_element_type=F32)
    mw_r, dl_r = {}, {}
    for ci in order:
        t_mw = jnp.zeros((8, L), F32)
        t_dl = jnp.zeros((8, L), F32)
        for h in hd:
            t_mw = jnp.where(sub8 == cs[h], mw[(ci, h)], t_mw)
            t_dl = jnp.where(sub8 == cs[h], dl[(ci, h)], t_dl)
        mw_r[ci], dl_r[ci] = t_mw, t_dl

    for ci in order:
        m_prev = m_sc[...]
        m_new = jnp.maximum(b_last[ci] + m_prev, mu[ci])
        decay = jnp.exp(b_last[ci] + m_prev - m_new)
        u_scale = jnp.exp(mu[ci] - m_new)
        m_sc[...] = m_new
        inter_log = b_r[ci] + m_prev
        m_row = jnp.maximum(inter_log, mw_r[ci])
        w_intra = jnp.exp(mw_r[ci] - m_row)
        w_cross = jnp.exp(inter_log - m_row)
        c_old = [c_sc[h] for h in hd]
        x = [jnp.dot(c_old[h].astype(BF16), qt_ref[hsl[h], rows[ci]], preferred_element_type=F32)
             for h in hd]
        for h in hd:
            c_sc[h] = (decay[cs[h]:cs[h] + 1, 0:1] * c_old[h]
                       + u_scale[cs[h]:cs[h] + 1, 0:1] * upd_sc[ci * HEADS + h])
        dc_r = jnp.zeros((8, L), F32)
        for h in hd:
            dc_r = jnp.where(sub8 == cs[h], x[h][HEAD_DIM:HEAD_DIM + 1, :], dc_r)
        den = w_intra * dl_r[ci] + w_cross * dc_r
        rden = 1.0 / jnp.maximum(jnp.abs(den), jnp.exp(-m_row))
        f_intra = w_intra * rden
        f_cross = w_cross * rden
        ht = [row(f_intra, cs[h]) * a_sc[ci * HEADS + h] + row(f_cross, cs[h]) * x[h][0:HEAD_DIM, :] for h in hd]
        if rev:
            hsum = [hft_ref[hsl[h], rows[ci]] + ht[h] for h in hd]
            inv = [lax.rsqrt(jnp.mean(hsum[h] * hsum[h], axis=0, keepdims=True) + EPS) for h in hd]
            for h in hd:
                mt = jax.nn.sigmoid(ot_ref[hsl[h], rows[ci]]) * (hsum[h] * inv[h] * hgt_ref[hsl[h], :])
                out_ref[rows[ci], hsl[h]] = mt.T.astype(out_ref.dtype)
        else:
            for h in hd:
                out_ref[hsl[h], rows[ci]] = ht[h]


def _mlstm(k, qt, vt, gt, tri, tri_t, B, S, rev, hft=None, ot=None, head_g_t=None):
    T = B * S
    nrow = CHUNK * MLSTM_CHUNKS_PER_STEP
    nc = S // nrow
    if rev:
        step = lambda b, c: b * nc + (nc - 1 - c)
    else:
        step = lambda b, c: b * nc + c
    tok = lambda b, c: (step(b, c), 0)
    tok_t = lambda b, c: (0, step(b, c))
    fix = lambda b, c: (0, 0)
    blk = lambda w: pl.BlockSpec((nrow, w), tok)
    blk_t = pl.BlockSpec((MLSTM_WIDTH, nrow), tok_t)
    in_specs = [blk(512), blk_t, blk_t, blk(LANES), pl.BlockSpec((CHUNK, CHUNK), fix),
                pl.BlockSpec((CHUNK, CHUNK), fix)]
    args = [k, qt, vt, gt, tri, tri_t]
    if rev:
        in_specs += [blk_t, blk_t, pl.BlockSpec((MLSTM_WIDTH, LANES), fix)]
        args += [hft, ot, head_g_t]
        out_spec, out_shape = blk(512), jax.ShapeDtypeStruct((T, MLSTM_WIDTH), BF16)
    else:
        out_spec, out_shape = blk_t, jax.ShapeDtypeStruct((MLSTM_WIDTH, T), F32)
    return pl.pallas_call(
        functools.partial(_mlstm_kernel, rev=rev),
        grid=(B, nc),
        in_specs=in_specs,
        out_specs=out_spec,
        out_shape=out_shape,
        scratch_shapes=[pltpu.VMEM((HEADS, STATE_ROWS, HEAD_DIM), F32), pltpu.VMEM((8, LANES), F32),
                        pltpu.VMEM((MLSTM_CHUNKS_PER_STEP * HEADS, HEAD_DIM, CHUNK), F32),
                        pltpu.VMEM((MLSTM_CHUNKS_PER_STEP * HEADS, STATE_ROWS, HEAD_DIM), F32)],
        compiler_params=_cparams(2),
        name="mlstm_bwd" if rev else "mlstm_fwd",
    )(*args)


def _mix_kernel(x_ref, u_ref, up_ref, un_ref, m_ref, bh_ref, bl_ref, pw_ref, ps_ref, wo_ref, g2_ref, wr_ref, br_ref,
                su_ref, cin_ref, x1_ref, xn_ref, eid_ref, rank_ref, ew_ref, cnt_ref, ext_sc, run_sc, *, seq_len):
    tm = TM_MIX
    i = pl.program_id(0)
    tps = seq_len // tm
    it = i % tps

    @pl.when(i == 0)
    def _():
        run_sc[...] = cin_ref[...]

    ext_sc[0:POOL_HALO, :] = jnp.where(it == 0, 0.0, up_ref[...]).astype(BF16)
    ext_sc[POOL_HALO:POOL_HALO + tm, :] = u_ref[...].astype(BF16)
    ext_sc[POOL_HALO + tm:, :] = jnp.where(it == tps - 1, 0.0, un_ref[...]).astype(BF16)
    nblk = tm // POOL_BLOCK
    ngrp = len(POOL_WINDOWS)
    cells = [(b, gi) for b in range(nblk) for gi in range(ngrp)]
    rs = [slice(b * POOL_BLOCK, (b + 1) * POOL_BLOCK) for b in range(nblk)]
    ls = [slice(gi * POOL_GROUP, (gi + 1) * POOL_GROUP) for gi in range(ngrp)]
    first = jnp.where(it == 0, 0, 1)
    final = jnp.where(it == tps - 1, 2, 1)

    def mean(b, gi):
        slab = ext_sc[b * POOL_BLOCK:(b + 1) * POOL_BLOCK + 2 * POOL_HALO, ls[gi]]
        if 0 < b < nblk - 1:
            return jnp.dot(bh_ref[1, gi], slab, preferred_element_type=F32)
        kind = first if b == 0 else final
        return (jnp.dot(bh_ref[kind, gi], slab, preferred_element_type=F32)
                + jnp.dot(bl_ref[kind, gi], slab, preferred_element_type=F32))

    means = {c: mean(*c) for c in cells}
    dev = {c: (means[c] - u_ref[rs[c[0]], ls[c[1]]]).astype(BF16) for c in cells}
    lin = {c: jnp.dot(dev[c], pw_ref[c[1]], preferred_element_type=F32) for c in cells}
    a = jnp.concatenate(
        [(jnp.concatenate([lin[(b, gi)] for gi in range(ngrp)], axis=1) * ps_ref[...]).astype(BF16)
         for b in range(nblk)], axis=0)

    x1 = (x_ref[...]
          + jnp.dot(a, wo_ref[0:POOL_WIDTH, :], preferred_element_type=F32)
          + jnp.dot(m_ref[...], wo_ref[POOL_WIDTH:, :], preferred_element_type=F32))
    x1_ref[...] = x1
    xn = x1 * lax.rsqrt(jnp.mean(x1 * x1, axis=-1, keepdims=True) + EPS) * g2_ref[...]
    _store_row_tiles(xn_ref, xn)

    lg = lax.dot_general(wr_ref[...], xn.astype(BF16), (((1,), (1,)), ((), ())),
                         preferred_element_type=F32) + br_ref[:, 0:1]
    gl = lg[0:N_GROUPS, :]
    gmax = jnp.max(gl, axis=0, keepdims=True)
    giota = lax.broadcasted_iota(I32, gl.shape, 0).astype(F32)
    g_idx = jnp.min(jnp.where(gl == gmax, giota, float(N_GROUPS)), axis=0, keepdims=True)
    g_w = 1.0 / jnp.sum(jnp.exp(gl - gmax), axis=0, keepdims=True)
    el = lg[8:8 + EPG, :]
    for gg in range(1, N_GROUPS):
        el = jnp.where(g_idx == float(gg), lg[8 + gg * EPG:8 + (gg + 1) * EPG, :], el)
    eiota = lax.broadcasted_iota(I32, el.shape, 0).astype(F32)
    top1 = jnp.max(el, axis=0, keepdims=True)
    i1 = jnp.min(jnp.where(el == top1, eiota, float(EPG)), axis=0, keepdims=True)
    el2 = jnp.where(eiota == i1, -jnp.inf, el)
    top2 = jnp.max(el2, axis=0, keepdims=True)
    i2 = jnp.min(jnp.where(el2 == top2, eiota, float(EPG)), axis=0, keepdims=True)
    ex = jnp.exp(top2 - top1)
    w1 = g_w / (1.0 + ex)
    w2 = g_w * ex / (1.0 + ex)
    e1 = g_idx * float(EPG) + i1
    e2 = g_idx * float(EPG) + i2
    eid_ref[0:1, :] = e1.astype(I32)
    eid_ref[1:2, :] = e2.astype(I32)

    xiota = lax.broadcasted_iota(I32, (N_EXPERTS, tm), 0).astype(F32)
    oh1 = xiota == e1
    oh2 = xiota == e2
    oh = jnp.where(oh1 | oh2, 1.0, 0.0)
    before = jnp.dot(oh.astype(BF16), su_ref[...], preferred_element_type=F32)
    base = run_sc[:, 0:1] + before
    rank_ref[0:1, :] = jnp.sum(jnp.where(oh1, base, 0.0), axis=0, keepdims=True).astype(I32)
    rank_ref[1:2, :] = jnp.sum(jnp.where(oh2, base, 0.0), axis=0, keepdims=True).astype(I32)
    run_new = run_sc[...] + jnp.sum(oh, axis=1, keepdims=True)
    run_sc[...] = run_new
    cnt_ref[...] = run_new

    riota = lax.broadcasted_iota(I32, (LANES, tm), 0)
    wt = jnp.where(riota == 0, w1, jnp.where(riota == 1, w2, 0.0))
    ew_ref[...] = wt.T


def _mix(x2, u, m, band_hi, band_lo, pool_w, pool_scale, w_out, norm2_g, wr, br, su, counts_in, seq_len):
    T = x2.shape[0]
    tm = TM_MIX
    assert seq_len % tm == 0 and seq_len >= 2 * POOL_BLOCK
    hb = tm // POOL_HALO
    nhb = T // POOL_HALO
    band_spec = pl.BlockSpec((3, len(POOL_WINDOWS), POOL_BLOCK, POOL_BLOCK + 2 * POOL_HALO), lambda i: (0, 0, 0, 0))
    tok = lambda i: (i, 0)
    fix = lambda i: (0, 0)
    outs = (
        jax.ShapeDtypeStruct((T, D_MODEL), F32),
        jax.ShapeDtypeStruct((T * ROW_SUB, LANES), F32),
        jax.ShapeDtypeStruct((2, T), I32),
        jax.ShapeDtypeStruct((2, T), I32),
        jax.ShapeDtypeStruct((T, LANES), F32),
        jax.ShapeDtypeStruct((N_EXPERTS, LANES), F32),
    )
    return pl.pallas_call(
        functools.partial(_mix_kernel, seq_len=seq_len),
        grid=(T // tm,),
        in_specs=[
            pl.BlockSpec((tm, D_MODEL), tok),
            pl.BlockSpec((tm, POOL_WIDTH), tok),
            pl.BlockSpec((POOL_HALO, POOL_WIDTH), lambda i: (jnp.maximum(i * hb - 1, 0), 0)),
            pl.BlockSpec((POOL_HALO, POOL_WIDTH), lambda i: (jnp.minimum((i + 1) * hb, nhb - 1), 0)),
            pl.BlockSpec((tm, MLSTM_WIDTH), tok),
            band_spec,
            band_spec,
            pl.BlockSpec((len(POOL_WINDOWS), POOL_GROUP, POOL_GROUP), lambda i: (0, 0, 0)),
            pl.BlockSpec((1, POOL_WIDTH), fix),
            pl.BlockSpec((D_MODEL, D_MODEL), fix),
            pl.BlockSpec((1, D_MODEL), fix),
            pl.BlockSpec((ROUTER_ROWS, D_MODEL), fix),
            pl.BlockSpec((ROUTER_ROWS, LANES), fix),
            pl.BlockSpec((tm, tm), fix),
            pl.BlockSpec((N_EXPERTS, LANES), fix),
        ],
        out_specs=[
            pl.BlockSpec((tm, D_MODEL), tok),
            pl.BlockSpec((tm * ROW_SUB, LANES), tok),
            pl.BlockSpec((2, tm), lambda i: (0, i)),
            pl.BlockSpec((2, tm), lambda i: (0, i)),
            pl.BlockSpec((tm, LANES), tok),
            pl.BlockSpec((N_EXPERTS, LANES), fix),
        ],
        out_shape=outs,
        scratch_shapes=[pltpu.VMEM((tm + 2 * POOL_HALO, POOL_WIDTH), BF16), pltpu.VMEM((N_EXPERTS, LANES), F32)],
        compiler_params=_cparams(1),
        name="mix_router",
    )(x2, u, u, u, m, band_hi, band_lo, pool_w, pool_scale, w_out, norm2_g, wr, br, su, counts_in)


def _row_copy(src, dst, sem):
    return pltpu.make_async_copy(src, dst, sem)


def _push_kernel(zs_ref, d0_ref, d1_ref, *refs, tiles):
    xn_refs = refs[:len(tiles)]
    rows_ref, zero_sc, sem, zsem = refs[len(tiles):]
    tm = TM_PUSH
    i = pl.program_id(0)

    @pl.when(i == 0)
    def _():
        zero_sc[...] = jnp.zeros_like(zero_sc)

        def fill(e):
            start = pl.multiple_of(jnp.maximum(zs_ref[e], 0) * ROW_SUB, ROW_SUB)
            return _row_copy(zero_sc, rows_ref.at[pl.ds(start, BM_EXPERT * ROW_SUB), :], zsem)

        for e in range(2 * N_EXPERTS):
            @pl.when(zs_ref[e] >= 0)
            def _():
                fill(e).start()
        for e in range(2 * N_EXPERTS):
            @pl.when(zs_ref[e] >= 0)
            def _():
                fill(e).wait()

    def push_tile(xn_ref):
        def copies(t):
            src = _row_tile(xn_ref, t)
            return (_row_copy(src, _row_tile(rows_ref, d0_ref[0, 0, t]), sem),
                    _row_copy(src, _row_tile(rows_ref, d1_ref[0, 0, t]), sem))

        def issue(t, carry):
            c0, c1 = copies(t)
            c0.start(priority=0)
            c1.start(priority=1)
            return carry

        def drain(t, carry):
            c0, c1 = copies(t)
            c0.wait()
            c1.wait()
            return carry

        lax.fori_loop(0, tm, issue, 0, unroll=4)
        lax.fori_loop(0, tm, drain, 0, unroll=4)

    first = 0
    for xn_ref, n in zip(xn_refs, tiles):
        @pl.when((i >= first) & (i < first + n))
        def _():
            push_tile(xn_ref)
        first += n


def _push(zero_start, dest0, dest1, xns, n_rows):
    tm = TM_PUSH
    tiles = tuple(xn.shape[0] // (ROW_SUB * tm) for xn in xns)
    smem_blk = pl.BlockSpec((1, 1, tm), lambda i, zs: (i, 0, 0), memory_space=pltpu.SMEM)
    in_specs = [smem_blk, smem_blk]
    first = 0
    for n in tiles:
        in_specs.append(pl.BlockSpec((tm * ROW_SUB, LANES),
                                     lambda i, zs, first=first, n=n: (jnp.clip(i - first, 0, n - 1), 0)))
        first += n
    grid_spec = pltpu.PrefetchScalarGridSpec(
        num_scalar_prefetch=1,
        grid=(sum(tiles),),
        in_specs=in_specs,
        out_specs=pl.BlockSpec(memory_space=pl.ANY),
        scratch_shapes=[pltpu.VMEM((BM_EXPERT * ROW_SUB, LANES), F32), pltpu.SemaphoreType.DMA(()),
                        pltpu.SemaphoreType.DMA(())],
    )
    return pl.pallas_call(
        functools.partial(_push_kernel, tiles=tiles),
        grid_spec=grid_spec,
        out_shape=jax.ShapeDtypeStruct((n_rows * ROW_SUB, LANES), F32),
        compiler_params=_cparams(1),
        name="row_push",
    )(zero_start, dest0, dest1, *xns)


def _expert_kernel(be_ref, nv_ref, x_ref, wg_ref, wu_ref, wd_ref, y_ref, wg_sc, wu_sc, wd_sc):
    i = pl.program_id(0)
    valid = i < nv_ref[0]
    new_expert = (i == 0) | (be_ref[i] != be_ref[jnp.maximum(i - 1, 0)])

    @pl.when(valid & new_expert)
    def _():
        wg_sc[...] = wg_ref[0].astype(BF16)
        wu_sc[...] = wu_ref[0].astype(BF16)
        wd_sc[...] = wd_ref[0].astype(BF16)

    @pl.when(valid)
    def _():
        xb = _load_row_tiles(x_ref).astype(BF16)
        gate = jnp.dot(xb, wg_sc[...], preferred_element_type=F32)
        up = jnp.dot(xb, wu_sc[...], preferred_element_type=F32)
        hb = (gate * jax.nn.sigmoid(gate) * up).astype(BF16)
        _store_row_tiles(y_ref, jnp.dot(hb, wd_sc[...], preferred_element_type=F32))

    @pl.when(jnp.logical_not(valid))
    def _():
        y_ref[...] = jnp.zeros_like(y_ref)


def _experts(block_e, n_valid, rows, w_gate, w_up, w_down):
    P = rows.shape[0] // ROW_SUB
    bm = BM_EXPERT
    last = lambda i, nv: jnp.minimum(i, jnp.maximum(nv[0] - 1, 0))
    row_blk = lambda i, be, nv: (last(i, nv), 0)
    w_blk = lambda i, be, nv: (be[last(i, nv)], 0, 0)
    grid_spec = pltpu.PrefetchScalarGridSpec(
        num_scalar_prefetch=2,
        grid=(P // bm,),
        in_specs=[
            pl.BlockSpec((bm * ROW_SUB, LANES), row_blk),
            pl.BlockSpec((1, D_MODEL, D_EXPERT), w_blk),
            pl.BlockSpec((1, D_MODEL, D_EXPERT), w_blk),
            pl.BlockSpec((1, D_EXPERT, D_MODEL), w_blk),
        ],
        out_specs=pl.BlockSpec((bm * ROW_SUB, LANES), lambda i, be, nv: (i, 0)),
        scratch_shapes=[pltpu.VMEM((D_MODEL, D_EXPERT), BF16), pltpu.VMEM((D_MODEL, D_EXPERT), BF16),
                        pltpu.VMEM((D_EXPERT, D_MODEL), BF16)],
    )
    return pl.pallas_call(
        _expert_kernel,
        grid_spec=grid_spec,
        out_shape=jax.ShapeDtypeStruct((P * ROW_SUB, LANES), F32),
        compiler_params=_cparams(1),
        name="experts",
    )(block_e, n_valid, rows, w_gate, w_up, w_down)


def _combine_kernel(d0_ref, d1_ref, d0n_ref, d1n_ref, x1_ref, ew_ref, gf_ref, y_ref, out_ref, buf, sem, *, n_tiles):
    tm = TM_GATHER
    i = pl.program_id(0)
    slot = i % 2

    def copies(d0, d1, sl, t):
        return (_row_copy(_row_tile(y_ref, d0[0, 0, t]), _row_tile(buf.at[sl, 0], t), sem.at[sl]),
                _row_copy(_row_tile(y_ref, d1[0, 0, t]), _row_tile(buf.at[sl, 1], t), sem.at[sl]))

    def gather_start(d0, d1, sl):
        def body(t, carry):
            c0, c1 = copies(d0, d1, sl, t)
            c0.start(priority=0)
            c1.start(priority=1)
            return carry
        lax.fori_loop(0, tm, body, 0, unroll=4)

    @pl.when(i == 0)
    def _():
        gather_start(d0_ref, d1_ref, 0)

    @pl.when(i + 1 < n_tiles)
    def _():
        gather_start(d0n_ref, d1n_ref, 1 - slot)

    def drain(t, carry):
        c0, c1 = copies(d0_ref, d1_ref, slot, t)
        c0.wait()
        c1.wait()
        return carry

    lax.fori_loop(0, tm, drain, 0, unroll=4)

    ew = ew_ref[...]
    x = (x1_ref[...] + _load_row_tiles(buf.at[slot, 0]) * ew[:, 0:1]
         + _load_row_tiles(buf.at[slot, 1]) * ew[:, 1:2])
    out_ref[...] = x * lax.rsqrt(jnp.mean(x * x, axis=-1, keepdims=True) + EPS) * gf_ref[...]


def _combine(dest0, dest1, x1, ew, final_g, yrows):
    T = x1.shape[0]
    tm = TM_GATHER
    nt = T // tm
    smem_blk = pl.BlockSpec((1, 1, tm), lambda i: (i, 0, 0), memory_space=pltpu.SMEM)
    smem_next = pl.BlockSpec((1, 1, tm), lambda i: (jnp.minimum(i + 1, nt - 1), 0, 0), memory_space=pltpu.SMEM)
    tok = lambda i: (i, 0)
    return pl.pallas_call(
        functools.partial(_combine_kernel, n_tiles=nt),
        grid=(nt,),
        in_specs=[smem_blk, smem_blk, smem_next, smem_next,
                  pl.BlockSpec((tm, D_MODEL), tok),
                  pl.BlockSpec((tm, LANES), tok),
                  pl.BlockSpec((1, D_MODEL), lambda i: (0, 0)),
                  pl.BlockSpec(memory_space=pl.ANY)],
        out_specs=pl.BlockSpec((tm, D_MODEL), tok),
        out_shape=jax.ShapeDtypeStruct((T, D_MODEL), F32),
        scratch_shapes=[pltpu.VMEM((2, 2, tm * ROW_SUB, LANES), F32), pltpu.SemaphoreType.DMA((2,))],
        compiler_params=_cparams(1),
        name="combine",
    )(dest0, dest1, dest0, dest1, x1, ew, final_g, yrows)


def _pool_bands():
    t = jnp.arange(POOL_BLOCK)[:, None]
    e = jnp.arange(POOL_BLOCK + 2 * POOL_HALO)[None, :]
    bands = []
    for kind in range(3):
        per_w = []
        for w in POOL_WINDOWS:
            half = w // 2
            win = (e >= t + POOL_HALO - half) & (e < t + POOL_HALO + half)
            lo_edge = jnp.maximum(t - half, 0) if kind == 0 else t - half
            hi_edge = jnp.minimum(t + half, POOL_BLOCK) if kind == 2 else t + half
            per_w.append(jnp.where(win, 1.0 / (hi_edge - lo_edge).astype(F32), 0.0))
        bands.append(jnp.stack(per_w))
    band = jnp.stack(bands)
    hi = band.astype(BF16)
    return hi, (band - hi.astype(F32)).astype(BF16)


def _prepare_params(norm1_g, w_in, pool_w, pool_scale, gate_b, head_norm_g, w_out, norm2_g,
                    router_group_w, router_group_b, router_expert_w, router_expert_b,
                    expert_w_gate, expert_w_up, expert_w_down, final_norm_g):
    w = w_in[0]
    n_main = 5 * 512
    w_main = jnp.concatenate([w[:, 0:512], w[:, 1024:1536]], axis=1).astype(BF16)
    w_feat_t = jnp.concatenate([w[:, 512:1024], w[:, 1536:n_main]], axis=1).T.astype(BF16)
    w_gate = jnp.pad(w[:, n_main:], ((0, 0), (0, LANES - N_GATES))).astype(BF16)
    gb = jnp.pad(gate_b[0], (0, LANES - N_GATES)).reshape(1, LANES)
    wr = jnp.zeros((ROUTER_ROWS, D_MODEL), F32)
    wr = wr.at[0:N_GROUPS].set(router_group_w[0].T)
    wr = wr.at[8:].set(router_expert_w[0].reshape(D_MODEL, N_EXPERTS).T)
    br = jnp.zeros((ROUTER_ROWS,), F32)
    br = br.at[0:N_GROUPS].set(router_group_b[0])
    br = br.at[8:].set(router_expert_b[0].reshape(N_EXPERTS))
    br = jnp.broadcast_to(br[:, None], (ROUTER_ROWS, LANES))
    idx = jnp.arange(CHUNK)
    tri_f = (idx[None, :] <= idx[:, None]).astype(F32)
    tri_b = (idx[None, :] >= idx[:, None]).astype(F32)
    tidx = jnp.arange(TM_MIX)
    su = (tidx[:, None] < tidx[None, :]).astype(BF16)
    band_hi, band_lo = _pool_bands()
    return dict(
        norm1_g=norm1_g[0].reshape(1, D_MODEL), w_main=w_main, w_feat_t=w_feat_t, w_gate=w_gate, gate_b=gb,
        pool_w=pool_w[0].astype(BF16), pool_scale=pool_scale[0].reshape(1, POOL_WIDTH),
        head_g_t=jnp.broadcast_to(head_norm_g[0][:, None], (MLSTM_WIDTH, LANES)), w_out=w_out[0].astype(BF16),
        norm2_g=norm2_g[0].reshape(1, D_MODEL), wr=wr.astype(BF16), br=br, tri_f=tri_f, tri_b=tri_b, su=su, band_hi=band_hi, band_lo=band_lo,
        e_gate=expert_w_gate[0], e_up=expert_w_up[0], e_down=expert_w_down[0],
        final_g=final_norm_g.reshape(1, D_MODEL),
    )


def _mixer(x, p, counts_in):
    B, S, _ = x.shape
    T = B * S
    x2 = x.reshape(T, D_MODEL)
    u, k, qt, vt, ot, gt = _inproj(x2, p["norm1_g"], p["w_main"], p["w_feat_t"], p["w_gate"], p["gate_b"])
    hft = _mlstm(k, qt, vt, gt, p["tri_f"], p["tri_b"], B, S, rev=False)
    m = _mlstm(k, qt, vt, gt, p["tri_b"], p["tri_f"], B, S, rev=True, hft=hft, ot=ot, head_g_t=p["head_g_t"])
    return _mix(x2, u, m, p["band_hi"], p["band_lo"], p["pool_w"], p["pool_scale"], p["w_out"], p["norm2_g"],
                p["wr"], p["br"], p["su"], counts_in, S)


def _encoders(xs, p):
    parts = []
    cnt = jnp.zeros((N_EXPERTS, LANES), F32)
    for x in xs:
        x1, xn, eid, rank, ew, cnt = _mixer(x, p, cnt)
        parts.append((x1, xn, eid, rank, ew))

    bm = BM_EXPERT
    n_tok = sum(part[0].shape[0] for part in parts)
    nb = (2 * n_tok) // bm + N_EXPERTS
    counts = cnt[:, 0].astype(I32)
    pcounts = ((counts + bm - 1) // bm) * bm
    pends = jnp.cumsum(pcounts)
    pstarts = pends - pcounts
    block_e = jnp.minimum(jnp.sum(pends[None, :] <= (jnp.arange(nb, dtype=I32) * bm)[:, None], axis=1),
                          N_EXPERTS - 1).astype(I32)
    n_valid = (pends[-1:] // bm).astype(I32)
    last_block = jnp.where(counts > 0, pends - bm, -1)
    tail_start = pends[-1] + jnp.arange(N_EXPERTS, dtype=I32) * bm
    tail_block = jnp.where(tail_start < nb * bm, tail_start, -1)
    zero_start = jnp.concatenate([last_block, tail_block]).astype(I32)
    e_ids = jnp.arange(N_EXPERTS, dtype=I32)[:, None, None]

    dests = []
    for x1, xn, eid, rank, ew in parts:
        dest = rank + jnp.sum(jnp.where(eid[None] == e_ids, pstarts[:, None, None], 0), axis=0)
        dests.append((dest[0].reshape(-1, 1, TM_GATHER), dest[1].reshape(-1, 1, TM_GATHER)))
    rows = _push(zero_start, jnp.concatenate([d[0].reshape(-1, 1, TM_PUSH) for d in dests]),
                 jnp.concatenate([d[1].reshape(-1, 1, TM_PUSH) for d in dests]),
                 [part[1] for part in parts], nb * bm)
    yrows = _experts(block_e, n_valid, rows, p["e_gate"], p["e_up"], p["e_down"])
    return tuple(_combine(d0, d1, x1, ew, p["final_g"], yrows).reshape(x.shape)
                 for x, (d0, d1), (x1, _, _, _, ew) in zip(xs, dests, parts))


def kernel(x_prompt, x_sample, norm1_g, w_in, pool_w, pool_scale, gate_b, head_norm_g, w_out, norm2_g,
           router_group_w, router_group_b, router_expert_w, router_expert_b,
           expert_w_gate, expert_w_up, expert_w_down, final_norm_g):
    p = _prepare_params(norm1_g, w_in, pool_w, pool_scale, gate_b, head_norm_g, w_out, norm2_g,
                        router_group_w, router_group_b, router_expert_w, router_expert_b,
                        expert_w_gate, expert_w_up, expert_w_down, final_norm_g)
    return _encoders((x_prompt, x_sample), p)
```
